```python
import math
import jax, jax.numpy as jnp
from jax import lax
import numpy as np

D_MODEL = 1024
BATCH = 8
SEQ = 2048
DEPTH = 1

N_MLA_HEADS = 8
D_MLA_NOPE = 64
D_MLA_ROPE = 32
D_MLA_V = 64
Q_LORA = 256
KV_LORA = 128
N_DIFF_HEADS = 8
D_DIFF_HEAD = 64
D_FF = 4 * D_MODEL
ROPE_THETA = 10000.0
Q_BLOCK = 128
EPS = 1e-6

C_QA = Q_LORA
C_KVA = KV_LORA + D_MLA_ROPE
C_DQ = 2 * N_DIFF_HEADS * D_DIFF_HEAD
C_DK = 2 * N_DIFF_HEADS * D_DIFF_HEAD
C_DV = N_DIFF_HEADS * 2 * D_DIFF_HEAD
C_GATE = 2 * D_MODEL
D_IN_TOTAL = C_QA + C_KVA + C_DQ + C_DK + C_DV + C_GATE
D_MLA_OUT = N_MLA_HEADS * D_MLA_V
D_DIFF_OUT = N_DIFF_HEADS * 2 * D_DIFF_HEAD

kernel_name = "hybrid_mla_diffattn_gated_sqrelu"


def rmsnorm(x, g):
    xf = x.astype(jnp.float32)
    y = xf * lax.rsqrt(jnp.mean(xf * xf, axis=-1, keepdims=True) + EPS)
    return (y * g.astype(jnp.float32)).astype(x.dtype)


def rope(x, pos):
    d = x.shape[-1]
    inv_freq = 1.0 / (ROPE_THETA ** (jnp.arange(0, d, 2, dtype=jnp.float32) / d))
    ang = pos[:, None] * inv_freq[None, :]
    cos = jnp.cos(ang)[None, :, None, :]
    sin = jnp.sin(ang)[None, :, None, :]
    xf = x.astype(jnp.float32)
    x1, x2 = xf[..., : d // 2], xf[..., d // 2:]
    out = jnp.concatenate([x1 * cos - x2 * sin, x2 * cos + x1 * sin], axis=-1)
    return out.astype(x.dtype)


def causal_softmax(q_blk, k_pre, q_start, scale):
    s = jnp.einsum('bqhd,bkhd->bhqk', q_blk, k_pre, preferred_element_type=jnp.float32) * scale
    nq, nk = q_blk.shape[1], k_pre.shape[1]
    mask = (q_start + jnp.arange(nq))[:, None] >= jnp.arange(nk)[None, :]
    s = jnp.where(mask[None, None], s, -jnp.inf)
    return jax.nn.softmax(s, axis=-1)


def block_causal_attention(q, k, v, scale, mix_probs):
    S = q.shape[1]
    outs = []
    for start in range(0, S, Q_BLOCK):
        end = start + Q_BLOCK
        p = causal_softmax(q[:, start:end], k[:, :end], start, scale)
        w = mix_probs(p).astype(v.dtype)
        outs.append(jnp.einsum('bhqk,bkhd->bqhd', w, v[:, :end]))
    return jnp.concatenate(outs, axis=1)


def setup_inputs(seed: int = 0) -> dict:
    key = jax.random.key(seed)
    ks = jax.random.split(key, 24)

    def dense(k, fan_in, fan_out):
        return jax.random.normal(k, (DEPTH, fan_in, fan_out), jnp.float32) * fan_in ** -0.5

    def gain(k, d):
        return 1.0 + 0.01 * jax.random.normal(k, (DEPTH, d), jnp.float32)

    return {
        "x": jax.random.normal(ks[0], (BATCH, SEQ, D_MODEL), jnp.float32),
        "pre_attn_g": gain(ks[1], D_MODEL),
        "w_in": dense(ks[2], D_MODEL, D_IN_TOTAL),
        "q_norm_g": gain(ks[3], Q_LORA),
        "w_q_b": dense(ks[4], Q_LORA, N_MLA_HEADS * (D_MLA_NOPE + D_MLA_ROPE)),
        "kv_norm_g": gain(ks[5], KV_LORA),
        "w_kv_b": dense(ks[6], KV_LORA, N_MLA_HEADS * (D_MLA_NOPE + D_MLA_V)),
        "lambda_q1": 0.1 * jax.random.normal(ks[7], (DEPTH, D_DIFF_HEAD), jnp.float32),
        "lambda_k1": 0.1 * jax.random.normal(ks[8], (DEPTH, D_DIFF_HEAD), jnp.float32),
        "lambda_q2": 0.1 * jax.random.normal(ks[9], (DEPTH, D_DIFF_HEAD), jnp.float32),
        "lambda_k2": 0.1 * jax.random.normal(ks[10], (DEPTH, D_DIFF_HEAD), jnp.float32),
        "subln_g": gain(ks[11], 2 * D_DIFF_HEAD),
        "w_br_mla": dense(ks[12], D_MLA_OUT, D_MODEL),
        "w_br_diff": dense(ks[13], D_DIFF_OUT, D_MODEL),
        "w_out": dense(ks[14], D_MODEL, D_MODEL),
        "post_attn_g": gain(ks[15], D_MODEL),
        "pre_mlp_g": gain(ks[16], D_MODEL),
        "w_mlp_up": dense(ks[17], D_MODEL, D_FF),
        "w_mlp_down": dense(ks[18], D_FF, D_MODEL),
        "post_mlp_g": gain(ks[19], D_MODEL),
    }


def reference(x, pre_attn_g, w_in, q_norm_g, w_q_b, kv_norm_g, w_kv_b,
              lambda_q1, lambda_k1, lambda_q2, lambda_k2, subln_g,
              w_br_mla, w_br_diff, w_out, post_attn_g, pre_mlp_g,
              w_mlp_up, w_mlp_down, post_mlp_g):
    B, S, _ = x.shape
    pos = jnp.arange(S, dtype=jnp.float32)
    splits = [C_QA, C_QA + C_KVA, C_QA + C_KVA + C_DQ, C_QA + C_KVA + C_DQ + C_DK,
              C_QA + C_KVA + C_DQ + C_DK + C_DV]
    mla_scale = 1.0 / math.sqrt(D_MLA_NOPE + D_MLA_ROPE)
    diff_scale = 1.0 / math.sqrt(D_DIFF_HEAD)

    for l in range(DEPTH):
        h = rmsnorm(x, pre_attn_g[l])
        proj = jnp.einsum('bsd,de->bse', h, w_in[l])
        qa, kva, dq, dk, dv, gate_logits = jnp.split(proj, splits, axis=-1)

        q = jnp.einsum('bsr,re->bse', rmsnorm(qa, q_norm_g[l]), w_q_b[l])
        q = q.reshape(B, S, N_MLA_HEADS, D_MLA_NOPE + D_MLA_ROPE)
        q_mla = jnp.concatenate([q[..., :D_MLA_NOPE], rope(q[..., D_MLA_NOPE:], pos)], axis=-1)
        c_kv, k_rope = kva[..., :KV_LORA], kva[..., KV_LORA:]
        k_rope = rope(k_rope[:, :, None, :], pos)
        kv = jnp.einsum('bsr,re->bse', rmsnorm(c_kv, kv_norm_g[l]), w_kv_b[l])
        kv = kv.reshape(B, S, N_MLA_HEADS, D_MLA_NOPE + D_MLA_V)
        k_nope, v_mla = kv[..., :D_MLA_NOPE], kv[..., D_MLA_NOPE:]
        k_mla = jnp.concatenate(
            [k_nope, jnp.broadcast_to(k_rope, (B, S, N_MLA_HEADS, D_MLA_ROPE))], axis=-1)
        o_mla = block_causal_attention(q_mla, k_mla, v_mla, mla_scale, lambda p: p)
        u_mla = jnp.einsum('bse,ed->bsd', o_mla.reshape(B, S, D_MLA_OUT), w_br_mla[l])

        lam_init = 0.8 - 0.6 * math.exp(-0.3 * l)
        lam = (jnp.exp(jnp.sum(lambda_q1[l].astype(jnp.float32) * lambda_k1[l].astype(jnp.float32)))
               - jnp.exp(jnp.sum(lambda_q2[l].astype(jnp.float32) * lambda_k2[l].astype(jnp.float32)))
               + lam_init)
        q_d = rope(dq.reshape(B, S, 2 * N_DIFF_HEADS, D_DIFF_HEAD), pos)
        k_d = rope(dk.reshape(B, S, 2 * N_DIFF_HEADS, D_DIFF_HEAD), pos)
        v_d = dv.reshape(B, S, N_DIFF_HEADS, 2 * D_DIFF_HEAD)

        def diff_mix(p):
            p = p.reshape(B, N_DIFF_HEADS, 2, p.shape[2], p.shape[3])
            return p[:, :, 0] - lam * p[:, :, 1]

        o_d = block_causal_attention(q_d, k_d, v_d, diff_scale, diff_mix)
        o_d = rmsnorm(o_d, subln_g[l]) * (1.0 - lam_init)
        u_diff = jnp.einsum('bse,ed->bsd', o_d.reshape(B, S, D_DIFF_OUT), w_br_diff[l])

        g_mla, g_diff = jnp.split(gate_logits, 2, axis=-1)
        mixed = jax.nn.sigmoid(g_mla) * u_mla + jax.nn.sigmoid(g_diff) * u_diff
        y = jnp.einsum('bsd,de->bse', mixed, w_out[l])
        x = x + rmsnorm(y, post_attn_g[l])

        h = rmsnorm(x, pre_mlp_g[l])
        m = jnp.square(jax.nn.relu(jnp.einsum('bsd,df->bsf', h, w_mlp_up[l])))
        m = jnp.einsum('bsf,fd->bsd', m, w_mlp_down[l])
        x = x + rmsnorm(m, post_mlp_g[l])
    return x
```

```python
import functools
import math

import numpy as np
import jax
import jax.numpy as jnp
from jax import lax
from jax.experimental import pallas as pl
from jax.experimental.pallas import tpu as pltpu

D_MODEL = 1024
N_MLA_HEADS = 8
D_MLA_NOPE = 64
D_MLA_ROPE = 32
D_MLA_V = 64
Q_LORA = 256
KV_LORA = 128
N_DIFF_HEADS = 8
D_DIFF_HEAD = 64
D_FF = 4 * D_MODEL
ROPE_THETA = 10000.0
EPS = 1e-6

LANES = 128
VMEM_LIMIT = 52 * 1024 * 1024

C_QA = Q_LORA
C_KVA = KV_LORA + D_MLA_ROPE
C_DQ = 2 * N_DIFF_HEADS * D_DIFF_HEAD
C_DK = C_DQ
C_DV = N_DIFF_HEADS * 2 * D_DIFF_HEAD
C_GATE = 2 * D_MODEL

O_QA = 0
O_CKV = O_QA + Q_LORA
O_KR = O_CKV + KV_LORA
O_DQ = O_KR + 2 * LANES
O_DK = O_DQ + C_DQ
O_DV = O_DK + C_DK
O_GATE = O_DV + C_DV
W_ALL_COLS = O_GATE + C_GATE

TM_PROJ = 256
TM_POST = 256
TQ = 256
TK = 256

F32 = jnp.float32
BF16 = jnp.bfloat16


def _rms(x, g):
    return x * lax.rsqrt(jnp.mean(x * x, axis=-1, keepdims=True) + EPS) * g


def _dot(a, b):
    return jnp.dot(a, b, preferred_element_type=F32)


def _dot_nt(a, b):
    return lax.dot_general(a, b, (((1,), (1,)), ((), ())), preferred_element_type=F32)


def _proj_kernel(x_ref, g_ref, w_ref, qng_ref, wqb_ref, kvg_ref, wkvb_ref,
                 cosd_ref, sind_ref, cosm_ref, sinm_ref,
                 qm_ref, km_ref, vm_ref, qd_ref, kd_ref, vd_ref, gate_ref,
                 *, mla_scale, diff_scale):
    h = _rms(x_ref[...], g_ref[...]).astype(BF16)

    def seg(a, b):
        return _dot(h, w_ref[:, a:b])

    cosm = cosm_ref[...]
    sinm = sinm_ref[...]

    qn = _rms(seg(O_QA, O_QA + Q_LORA), qng_ref[...]).astype(BF16)
    qab = _dot(qn, wqb_ref[...])
    half = N_MLA_HEADS * LANES
    for hh in range(N_MLA_HEADS):
        a = qab[:, hh * LANES:(hh + 1) * LANES]
        b = qab[:, half + hh * LANES:half + (hh + 1) * LANES]
        qm_ref[:, hh * LANES:(hh + 1) * LANES] = ((a * cosm + b * sinm) * mla_scale).astype(BF16)

    kn = _rms(seg(O_CKV, O_CKV + KV_LORA), kvg_ref[...]).astype(BF16)
    kvb = _dot(kn, wkvb_ref[...])
    kr = seg(O_KR, O_KR + 2 * LANES)
    krp = kr[:, :LANES] * cosm + kr[:, LANES:] * sinm
    for hh in range(N_MLA_HEADS):
        km_ref[:, hh * LANES:(hh + 1) * LANES] = (kvb[:, hh * LANES:(hh + 1) * LANES] + krp).astype(BF16)
    vm_ref[...] = kvb[:, half:].astype(BF16)

    cosd = cosd_ref[...]
    sind = sind_ref[...]
    lane = lax.broadcasted_iota(jnp.int32, (1, LANES), 1)
    is_a = (lane // (D_DIFF_HEAD // 2)) % 2 == 0
    mask_a = jnp.where(is_a, diff_scale, 0.0).astype(F32)
    mask_b = jnp.where(is_a, 0.0, diff_scale).astype(F32)

    def rope_group(xg):
        return xg * cosd + pltpu.roll(xg, LANES // 2, 1) * sind

    dq = seg(O_DQ, O_DQ + C_DQ)
    for j in range(N_DIFF_HEADS):
        r = rope_group(dq[:, j * LANES:(j + 1) * LANES])
        qd_ref[:, 2 * j * LANES:(2 * j + 1) * LANES] = (r * mask_a).astype(BF16)
        qd_ref[:, (2 * j + 1) * LANES:(2 * j + 2) * LANES] = (r * mask_b).astype(BF16)
    dk = seg(O_DK, O_DK + C_DK)
    for j in range(N_DIFF_HEADS):
        kd_ref[:, j * LANES:(j + 1) * LANES] = rope_group(dk[:, j * LANES:(j + 1) * LANES]).astype(BF16)

    vd_ref[...] = seg(O_DV, O_DV + C_DV).astype(BF16)
    gl = seg(O_GATE, O_GATE + C_GATE)
    gate_ref[...] = (1.0 / (1.0 + jnp.exp(-gl))).astype(BF16)


def _const_spec(shape):
    return pl.BlockSpec(shape, lambda i: (0,) * len(shape), pipeline_mode=pl.Buffered(1))


def _proj_call(x2, g, w_all, qng, wqb, kvg, wkvb, cosd, sind, cosm, sinm, seq):
    t = x2.shape[0]
    tm = TM_PROJ
    tiles_per_seq = seq // tm
    row = lambda c: pl.BlockSpec((tm, c), lambda i: (i, 0))
    tab = pl.BlockSpec((tm, LANES), lambda i: (i % tiles_per_seq, 0))
    widths = (N_MLA_HEADS * LANES, N_MLA_HEADS * LANES, N_MLA_HEADS * D_MLA_V,
              2 * C_DQ, C_DK, C_DV, C_GATE)
    kern = functools.partial(
        _proj_kernel,
        mla_scale=1.0 / math.sqrt(D_MLA_NOPE + D_MLA_ROPE),
        diff_scale=1.0 / math.sqrt(D_DIFF_HEAD))
    return pl.pallas_call(
        kern,
        grid=(t // tm,),
        in_specs=[row(D_MODEL), _const_spec((1, D_MODEL)), _const_spec(w_all.shape),
                  _const_spec((1, Q_LORA)), _const_spec(wqb.shape),
                  _const_spec((1, KV_LORA)), _const_spec(wkvb.shape),
                  tab, tab, tab, tab],
        out_specs=[row(c) for c in widths],
        out_shape=[jax.ShapeDtypeStruct((t, c), BF16) for c in widths],
        compiler_params=pltpu.CompilerParams(
            dimension_semantics=("arbitrary",), vmem_limit_bytes=VMEM_LIMIT),
        name="proj",
    )(x2, g, w_all, qng, wqb, kvg, wkvb, cosd, sind, cosm, sinm)


def _flash_head(q, k_ref, v_ref, k_lo, qi):
    def scores(start):
        return _dot_nt(q, k_ref[pl.ds(start, TK), k_lo:k_lo + LANES])

    start = pl.multiple_of(qi * TQ, TQ)
    s = scores(start)
    r = lax.broadcasted_iota(jnp.int32, (TQ, TK), 0)
    c = lax.broadcasted_iota(jnp.int32, (TQ, TK), 1)
    s = jnp.where(r >= c, s, -jnp.inf)
    m = jnp.max(s, axis=-1, keepdims=True)
    p = jnp.exp(s - m)
    l = jnp.sum(p, axis=-1, keepdims=True)
    acc = _dot(p.astype(BF16), v_ref[pl.ds(start, TK), :])

    def body(kb, carry):
        m, l, acc = carry
        st = pl.multiple_of(kb * TK, TK)
        s = scores(st)
        m_new = jnp.maximum(m, jnp.max(s, axis=-1, keepdims=True))
        alpha = jnp.exp(m - m_new)
        p = jnp.exp(s - m_new)
        l = alpha * l + jnp.sum(p, axis=-1, keepdims=True)
        acc = alpha * acc + _dot(p.astype(BF16), v_ref[pl.ds(st, TK), :])
        return m_new, l, acc

    m, l, acc = lax.fori_loop(0, qi, body, (m, l, acc))
    return acc / l


def _mla_attn_kernel(q_ref, k_ref, v_ref, o_ref):
    qi = pl.program_id(2)
    o0 = _flash_head(q_ref[:, :LANES], k_ref, v_ref, 0, qi)
    o1 = _flash_head(q_ref[:, LANES:], k_ref, v_ref, LANES, qi)
    lane = lax.broadcasted_iota(jnp.int32, (TQ, LANES), 1)
    o_ref[...] = jnp.where(lane < D_MLA_V, o0, o1).astype(BF16)


def _diff_attn_kernel(lam_ref, sg_ref, q_ref, k_ref, v_ref, o_ref, *, lam_init):
    qi = pl.program_id(2)
    lp = lam_ref[...]
    lam = (jnp.exp(jnp.sum(lp[0:1] * lp[1:2], axis=-1, keepdims=True))
           - jnp.exp(jnp.sum(lp[2:3] * lp[3:4], axis=-1, keepdims=True)) + lam_init)
    o1 = _flash_head(q_ref[:, :LANES], k_ref, v_ref, 0, qi)
    o2 = _flash_head(q_ref[:, LANES:], k_ref, v_ref, 0, qi)
    o = o1 - lam * o2
    o_ref[...] = (_rms(o, sg_ref[...]) * (1.0 - lam_init)).astype(BF16)


def _attn_params():
    return pltpu.CompilerParams(
        dimension_semantics=("arbitrary", "arbitrary", "arbitrary"), vmem_limit_bytes=VMEM_LIMIT)


def _mla_attn_call(qm, km, vm, batch, seq):
    t = qm.shape[0]
    nq = seq // TQ
    pairs = N_MLA_HEADS // 2
    return pl.pallas_call(
        _mla_attn_kernel,
        grid=(batch, pairs, nq),
        in_specs=[pl.BlockSpec((TQ, 2 * LANES), lambda b, p, qi: (b * nq + qi, p)),
                  pl.BlockSpec((seq, 2 * LANES), lambda b, p, qi: (b, p)),
                  pl.BlockSpec((seq, LANES), lambda b, p, qi: (b, p))],
        out_specs=pl.BlockSpec((TQ, LANES), lambda b, p, qi: (b * nq + qi, p)),
        out_shape=jax.ShapeDtypeStruct((t, N_MLA_HEADS * D_MLA_V), BF16),
        compiler_params=_attn_params(),
        name="mla_attn",
    )(qm, km, vm)


def _diff_attn_call(lam_p, sg, qd, kd, vd, batch, seq, lam_init):
    t = qd.shape[0]
    nq = seq // TQ
    return pl.pallas_call(
        functools.partial(_diff_attn_kernel, lam_init=lam_init),
        grid=(batch, N_DIFF_HEADS, nq),
        in_specs=[pl.BlockSpec(lam_p.shape, lambda b, j, qi: (0, 0)),
                  pl.BlockSpec(sg.shape, lambda b, j, qi: (0, 0)),
                  pl.BlockSpec((TQ, 2 * LANES), lambda b, j, qi: (b * nq + qi, j)),
                  pl.BlockSpec((seq, LANES), lambda b, j, qi: (b, j)),
                  pl.BlockSpec((seq, LANES), lambda b, j, qi: (b, j))],
        out_specs=pl.BlockSpec((TQ, LANES), lambda b, j, qi: (b * nq + qi, j)),
        out_shape=jax.ShapeDtypeStruct((t, C_DV), BF16),
        compiler_params=_attn_params(),
        name="diff_attn",
    )(lam_p, sg, qd, kd, vd)


def _post_kernel(x_ref, om_ref, od_ref, gate_ref, wbm_ref, wbd_ref, wo_ref, gpa_ref, gpm_ref,
                 wup_ref, wdn_ref, gpo_ref, o_ref):
    u_mla = _dot(om_ref[...], wbm_ref[...])
    u_diff = _dot(od_ref[...], wbd_ref[...])
    g = gate_ref[...].astype(F32)
    mixed = (g[:, :D_MODEL] * u_mla + g[:, D_MODEL:] * u_diff).astype(BF16)
    y = _dot(mixed, wo_ref[...])
    x1 = x_ref[...] + _rms(y, gpa_ref[...])
    h = _rms(x1, gpm_ref[...]).astype(BF16)
    up = jnp.maximum(_dot(h, wup_ref[...]), 0.0)
    m = _dot((up * up).astype(BF16), wdn_ref[...])
    o_ref[...] = x1 + _rms(m, gpo_ref[...])


def _post_call(x2, om, od, gate, wbm, wbd, wo, gpa, gpm, wup, wdn, gpo):
    t = x2.shape[0]
    tm = TM_POST
    row = lambda c: pl.BlockSpec((tm, c), lambda i: (i, 0))
    return pl.pallas_call(
        _post_kernel,
        grid=(t // tm,),
        in_specs=[row(D_MODEL), row(om.shape[1]), row(od.shape[1]), row(gate.shape[1]),
                  _const_spec(wbm.shape), _const_spec(wbd.shape), _const_spec(wo.shape),
                  _const_spec((1, D_MODEL)), _const_spec((1, D_MODEL)),
                  _const_spec(wup.shape), _const_spec(wdn.shape), _const_spec((1, D_MODEL))],
        out_specs=row(D_MODEL),
        out_shape=jax.ShapeDtypeStruct((t, D_MODEL), F32),
        compiler_params=pltpu.CompilerParams(
            dimension_semantics=("arbitrary",), vmem_limit_bytes=VMEM_LIMIT),
        name="post",
    )(x2, om, od, gate, wbm, wbd, wo, gpa, gpm, wup, wdn, gpo)


def _diff_perm():
    half = D_DIFF_HEAD // 2
    idx = np.zeros(C_DQ, np.int32)
    for j in range(N_DIFF_HEADS):
        for ln in range(LANES):
            head = 2 * j + (ln // half) % 2
            tdim = ln % half + half * (ln // (2 * half))
            idx[j * LANES + ln] = head * D_DIFF_HEAD + tdim
    return idx


def _rot_half_cols(w):
    d = w.shape[-1]
    return jnp.concatenate([-w[..., d // 2:], w[..., :d // 2]], axis=-1)


def _rope_tables(seq):
    pos = jnp.arange(seq, dtype=F32)

    def cs(d):
        inv = 1.0 / (ROPE_THETA ** (jnp.arange(0, d, 2, dtype=F32) / d))
        ang = pos[:, None] * inv[None, :]
        return jnp.cos(ang), jnp.sin(ang)

    cd, sd = cs(D_DIFF_HEAD)
    cosd = jnp.tile(cd, (1, 4))
    sind = jnp.concatenate([-sd, -sd, sd, sd], axis=1)
    cm, sm = cs(D_MLA_ROPE)
    one = jnp.ones((seq, D_MLA_NOPE), F32)
    zero_n = jnp.zeros((seq, D_MLA_NOPE), F32)
    zero_p = jnp.zeros((seq, LANES - D_MLA_NOPE - D_MLA_ROPE), F32)
    cosm = jnp.concatenate([one, cm, cm, zero_p], axis=1)
    sinm = jnp.concatenate([zero_n, sm, sm, zero_p], axis=1)
    return cosd, sind, cosm, sinm


def _layout_in_proj(w_in):
    d = w_in.shape[0]
    s0 = C_QA
    s1 = s0 + C_KVA
    s2 = s1 + C_DQ
    s3 = s2 + C_DK
    s4 = s3 + C_DV
    w_qa = w_in[:, :s0]
    w_ckv = w_in[:, s0:s0 + KV_LORA]
    w_kr = w_in[:, s0 + KV_LORA:s1]
    zn = jnp.zeros((d, D_MLA_NOPE), w_in.dtype)
    zp = jnp.zeros((d, LANES - D_MLA_NOPE - D_MLA_ROPE), w_in.dtype)
    g1 = jnp.concatenate([zn, w_kr, zp], axis=1)
    g2 = jnp.concatenate([zn, _rot_half_cols(w_kr), zp], axis=1)
    perm = _diff_perm()
    w_dq = jnp.take(w_in[:, s1:s2], perm, axis=1)
    w_dk = jnp.take(w_in[:, s2:s3], perm, axis=1)
    w_all = jnp.concatenate([w_qa, w_ckv, g1, g2, w_dq, w_dk, w_in[:, s3:s4], w_in[:, s4:]], axis=1)
    assert w_all.shape[1] == W_ALL_COLS
    return w_all.astype(BF16)


def _layout_q_b(w_q_b):
    r = w_q_b.shape[0]
    w = w_q_b.reshape(r, N_MLA_HEADS, D_MLA_NOPE + D_MLA_ROPE)
    nope, rp = w[..., :D_MLA_NOPE], w[..., D_MLA_NOPE:]
    zp = jnp.zeros((r, N_MLA_HEADS, LANES - D_MLA_NOPE - D_MLA_ROPE), w.dtype)
    zn = jnp.zeros((r, N_MLA_HEADS, D_MLA_NOPE), w.dtype)
    a = jnp.concatenate([nope, rp, zp], axis=-1).reshape(r, N_MLA_HEADS * LANES)
    b = jnp.concatenate([zn, _rot_half_cols(rp), zp], axis=-1).reshape(r, N_MLA_HEADS * LANES)
    return jnp.concatenate([a, b], axis=1).astype(BF16)


def _layout_kv_b(w_kv_b):
    r = w_kv_b.shape[0]
    w = w_kv_b.reshape(r, N_MLA_HEADS, D_MLA_NOPE + D_MLA_V)
    kn, v = w[..., :D_MLA_NOPE], w[..., D_MLA_NOPE:]
    zk = jnp.zeros((r, N_MLA_HEADS, LANES - D_MLA_NOPE), w.dtype)
    wk = jnp.concatenate([kn, zk], axis=-1).reshape(r, N_MLA_HEADS * LANES)
    wv = v.reshape(r, N_MLA_HEADS * D_MLA_V)
    return jnp.concatenate([wk, wv], axis=1).astype(BF16)


def kernel(x, pre_attn_g, w_in, q_norm_g, w_q_b, kv_norm_g, w_kv_b, lambda_q1, lambda_k1, lambda_q2, lambda_k2, subln_g, w_br_mla, w_br_diff, w_out, post_attn_g, pre_mlp_g, w_mlp_up, w_mlp_down, post_mlp_g):
    batch, seq, d_model = x.shape
    depth = w_in.shape[0]
    assert d_model == D_MODEL and seq % TQ == 0 and seq % TM_PROJ == 0
    assert (batch * seq) % TM_POST == 0
    cosd, sind, cosm, sinm = _rope_tables(seq)
    x2 = x.reshape(batch * seq, d_model)
    row = lambda v: v.reshape(1, -1).astype(F32)
    for l in range(depth):
        lam_init = 0.8 - 0.6 * math.exp(-0.3 * l)
        qm, km, vm, qd, kd, vd, gate = _proj_call(
            x2, row(pre_attn_g[l]), _layout_in_proj(w_in[l]), row(q_norm_g[l]),
            _layout_q_b(w_q_b[l]), row(kv_norm_g[l]), _layout_kv_b(w_kv_b[l]),
            cosd, sind, cosm, sinm, seq)
        om = _mla_attn_call(qm, km, vm, batch, seq)
        lam_p = jnp.stack([lambda_q1[l], lambda_k1[l], lambda_q2[l], lambda_k2[l]]).astype(F32)
        od = _diff_attn_call(lam_p, row(subln_g[l]), qd, kd, vd, batch, seq, lam_init)
        x2 = _post_call(
            x2, om, od, gate, w_br_mla[l].astype(BF16), w_br_diff[l].astype(BF16),
            w_out[l].astype(BF16), row(post_attn_g[l]), row(pre_mlp_g[l]),
            w_mlp_up[l].astype(BF16), w_mlp_down[l].astype(BF16), row(post_mlp_g[l]))
    return x2.reshape(batch, seq, d_model)
```

```python
import functools
import math

import numpy as np
import jax
import jax.numpy as jnp
from jax import lax
from jax.experimental import pallas as pl
from jax.experimental.pallas import tpu as pltpu

D_MODEL = 1024
N_MLA_HEADS = 8
D_MLA_NOPE = 64
D_MLA_ROPE = 32
D_MLA_V = 64
Q_LORA = 256
KV_LORA = 128
N_DIFF_HEADS = 8
D_DIFF_HEAD = 64
D_FF = 4 * D_MODEL
ROPE_THETA = 10000.0
EPS = 1e-6

LANES = 128
VMEM_LIMIT = 52 * 1024 * 1024

C_QA = Q_LORA
C_KVA = KV_LORA + D_MLA_ROPE
C_DQ = 2 * N_DIFF_HEADS * D_DIFF_HEAD
C_DK = C_DQ
C_DV = N_DIFF_HEADS * 2 * D_DIFF_HEAD
C_GATE = 2 * D_MODEL

O_QA = 0
O_CKV = O_QA + Q_LORA
O_KR = O_CKV + KV_LORA
O_DQ = O_KR + 2 * LANES
O_DK = O_DQ + C_DQ
O_DV = O_DK + C_DK
O_GATE = O_DV + C_DV
W_ALL_COLS = O_GATE + C_GATE

TM_PROJ = 256
TM_POST = 256
TQ = 512
TK = 512

LOG2E = math.log2(math.e)

F32 = jnp.float32
BF16 = jnp.bfloat16


def _rms(x, g):
    return x * lax.rsqrt(jnp.mean(x * x, axis=-1, keepdims=True) + EPS) * g


def _dot(a, b):
    return jnp.dot(a, b, preferred_element_type=F32)


def _dot_nt(a, b):
    return lax.dot_general(a, b, (((1,), (1,)), ((), ())), preferred_element_type=F32)


def _proj_kernel(x_ref, g_ref, w_ref, qng_ref, wqb_ref, kvg_ref, wkvb_ref,
                 cosd_ref, sind_ref, cosm_ref, sinm_ref,
                 qm_ref, km_ref, vm_ref, qd_ref, kd_ref, vd_ref, gate_ref,
                 *, mla_scale, diff_scale):
    h = _rms(x_ref[...], g_ref[...]).astype(BF16)

    def seg(a, b):
        return _dot(h, w_ref[:, a:b])

    cosm = cosm_ref[...]
    sinm = sinm_ref[...]

    qn = _rms(seg(O_QA, O_QA + Q_LORA), qng_ref[...]).astype(BF16)
    qab = _dot(qn, wqb_ref[...])
    half = N_MLA_HEADS * LANES
    for hh in range(N_MLA_HEADS):
        a = qab[:, hh * LANES:(hh + 1) * LANES]
        b = qab[:, half + hh * LANES:half + (hh + 1) * LANES]
        qm_ref[:, hh * LANES:(hh + 1) * LANES] = ((a * cosm + b * sinm) * mla_scale).astype(BF16)

    kn = _rms(seg(O_CKV, O_CKV + KV_LORA), kvg_ref[...]).astype(BF16)
    kvb = _dot(kn, wkvb_ref[...])
    kr = seg(O_KR, O_KR + 2 * LANES)
    krp = kr[:, :LANES] * cosm + kr[:, LANES:] * sinm
    for hh in range(N_MLA_HEADS):
        km_ref[:, hh * LANES:(hh + 1) * LANES] = (kvb[:, hh * LANES:(hh + 1) * LANES] + krp).astype(BF16)
    vm_ref[...] = kvb[:, half:].astype(BF16)

    cosd = cosd_ref[...]
    sind = sind_ref[...]
    lane = lax.broadcasted_iota(jnp.int32, (1, LANES), 1)
    is_a = (lane // (D_DIFF_HEAD // 2)) % 2 == 0
    mask_a = jnp.where(is_a, diff_scale, 0.0).astype(F32)
    mask_b = jnp.where(is_a, 0.0, diff_scale).astype(F32)

    def rope_group(xg):
        return xg * cosd + pltpu.roll(xg, LANES // 2, 1) * sind

    dq = seg(O_DQ, O_DQ + C_DQ)
    for j in range(N_DIFF_HEADS):
        r = rope_group(dq[:, j * LANES:(j + 1) * LANES])
        qd_ref[:, 2 * j * LANES:(2 * j + 1) * LANES] = (r * mask_a).astype(BF16)
        qd_ref[:, (2 * j + 1) * LANES:(2 * j + 2) * LANES] = (r * mask_b).astype(BF16)
    dk = seg(O_DK, O_DK + C_DK)
    for j in range(N_DIFF_HEADS):
        kd_ref[:, j * LANES:(j + 1) * LANES] = rope_group(dk[:, j * LANES:(j + 1) * LANES]).astype(BF16)

    vd_ref[...] = seg(O_DV, O_DV + C_DV).astype(BF16)
    gl = seg(O_GATE, O_GATE + C_GATE)
    gate_ref[...] = (1.0 / (1.0 + jnp.exp(-gl))).astype(BF16)


def _const_spec(shape):
    return pl.BlockSpec(shape, lambda i: (0,) * len(shape), pipeline_mode=pl.Buffered(1))


def _proj_call(x2, g, w_all, qng, wqb, kvg, wkvb, cosd, sind, cosm, sinm, seq):
    t = x2.shape[0]
    tm = TM_PROJ
    tiles_per_seq = seq // tm
    row = lambda c: pl.BlockSpec((tm, c), lambda i: (i, 0))
    tab = pl.BlockSpec((tm, LANES), lambda i: (i % tiles_per_seq, 0))
    widths = (N_MLA_HEADS * LANES, N_MLA_HEADS * LANES, N_MLA_HEADS * D_MLA_V,
              2 * C_DQ, C_DK, C_DV, C_GATE)
    kern = functools.partial(
        _proj_kernel,
        mla_scale=LOG2E / math.sqrt(D_MLA_NOPE + D_MLA_ROPE),
        diff_scale=LOG2E / math.sqrt(D_DIFF_HEAD))
    return pl.pallas_call(
        kern,
        grid=(t // tm,),
        in_specs=[row(D_MODEL), _const_spec((1, D_MODEL)), _const_spec(w_all.shape),
                  _const_spec((1, Q_LORA)), _const_spec(wqb.shape),
                  _const_spec((1, KV_LORA)), _const_spec(wkvb.shape),
                  tab, tab, tab, tab],
        out_specs=[row(c) for c in widths],
        out_shape=[jax.ShapeDtypeStruct((t, c), BF16) for c in widths],
        compiler_params=pltpu.CompilerParams(
            dimension_semantics=("arbitrary",), vmem_limit_bytes=VMEM_LIMIT),
        name="proj",
    )(x2, g, w_all, qng, wqb, kvg, wkvb, cosd, sind, cosm, sinm)


def _flash_heads(qs, k_los, k_ref, v_ref, qi):
    n = len(qs)

    def scores(i, start):
        return _dot_nt(qs[i], k_ref[pl.ds(start, TK), k_los[i]:k_los[i] + LANES])

    start = pl.multiple_of(qi * TQ, TQ)
    r = lax.broadcasted_iota(jnp.int32, (TQ, TK), 0)
    c = lax.broadcasted_iota(jnp.int32, (TQ, TK), 1)
    keep = r >= c
    v0 = v_ref[pl.ds(start, TK), :]
    init = []
    for i in range(n):
        s = jnp.where(keep, scores(i, start), -jnp.inf)
        m = jnp.max(s, axis=-1, keepdims=True)
        p = jnp.exp2(s - m)
        l = jnp.sum(p, axis=-1, keepdims=True)
        init += [m, l, _dot(p.astype(BF16), v0)]

    def body(kb, carry):
        st = pl.multiple_of(kb * TK, TK)
        v = v_ref[pl.ds(st, TK), :]
        out = []
        for i in range(n):
            m, l, acc = carry[3 * i:3 * i + 3]
            s = scores(i, st)
            m_new = jnp.maximum(m, jnp.max(s, axis=-1, keepdims=True))
            alpha = jnp.exp2(m - m_new)
            p = jnp.exp2(s - m_new)
            l = alpha * l + jnp.sum(p, axis=-1, keepdims=True)
            acc = alpha * acc + _dot(p.astype(BF16), v)
            out += [m_new, l, acc]
        return tuple(out)

    fin = lax.fori_loop(0, qi, body, tuple(init))
    return [fin[3 * i + 2] / fin[3 * i + 1] for i in range(n)]


def _mla_attn_kernel(q_ref, k_ref, v_ref, o_ref):
    qi = pl.program_id(2)
    o0, o1 = _flash_heads([q_ref[:, :LANES], q_ref[:, LANES:]], [0, LANES], k_ref, v_ref, qi)
    lane = lax.broadcasted_iota(jnp.int32, (TQ, LANES), 1)
    o_ref[...] = jnp.where(lane < D_MLA_V, o0, o1).astype(BF16)


def _diff_attn_kernel(lam_ref, sg_ref, q_ref, k_ref, v_ref, o_ref, *, lam_init):
    qi = pl.program_id(2)
    lp = lam_ref[...]
    lam = (jnp.exp(jnp.sum(lp[0:1] * lp[1:2], axis=-1, keepdims=True))
           - jnp.exp(jnp.sum(lp[2:3] * lp[3:4], axis=-1, keepdims=True)) + lam_init)
    o1, o2 = _flash_heads([q_ref[:, :LANES], q_ref[:, LANES:]], [0, 0], k_ref, v_ref, qi)
    o = o1 - lam * o2
    o_ref[...] = (_rms(o, sg_ref[...]) * (1.0 - lam_init)).astype(BF16)


def _attn_params():
    return pltpu.CompilerParams(
        dimension_semantics=("arbitrary", "arbitrary", "arbitrary"), vmem_limit_bytes=VMEM_LIMIT)


def _mla_attn_call(qm, km, vm, batch, seq):
    t = qm.shape[0]
    nq = seq // TQ
    pairs = N_MLA_HEADS // 2
    return pl.pallas_call(
        _mla_attn_kernel,
        grid=(batch, pairs, nq),
        in_specs=[pl.BlockSpec((TQ, 2 * LANES), lambda b, p, qi: (b * nq + qi, p)),
                  pl.BlockSpec((seq, 2 * LANES), lambda b, p, qi: (b, p)),
                  pl.BlockSpec((seq, LANES), lambda b, p, qi: (b, p))],
        out_specs=pl.BlockSpec((TQ, LANES), lambda b, p, qi: (b * nq + qi, p)),
        out_shape=jax.ShapeDtypeStruct((t, N_MLA_HEADS * D_MLA_V), BF16),
        compiler_params=_attn_params(),
        name="mla_attn",
    )(qm, km, vm)


def _diff_attn_call(lam_p, sg, qd, kd, vd, batch, seq, lam_init):
    t = qd.shape[0]
    nq = seq // TQ
    return pl.pallas_call(
        functools.partial(_diff_attn_kernel, lam_init=lam_init),
        grid=(batch, N_DIFF_HEADS, nq),
        in_specs=[pl.BlockSpec(lam_p.shape, lambda b, j, qi: (0, 0)),
                  pl.BlockSpec(sg.shape, lambda b, j, qi: (0, 0)),
                  pl.BlockSpec((TQ, 2 * LANES), lambda b, j, qi: (b * nq + qi, j)),
                  pl.BlockSpec((seq, LANES), lambda b, j, qi: (b, j)),
                  pl.BlockSpec((seq, LANES), lambda b, j, qi: (b, j))],
        out_specs=pl.BlockSpec((TQ, LANES), lambda b, j, qi: (b * nq + qi, j)),
        out_shape=jax.ShapeDtypeStruct((t, C_DV), BF16),
        compiler_params=_attn_params(),
        name="diff_attn",
    )(lam_p, sg, qd, kd, vd)


def _post_kernel(x_ref, om_ref, od_ref, gate_ref, wbm_ref, wbd_ref, wo_ref, gpa_ref, gpm_ref,
                 wup_ref, wdn_ref, gpo_ref, o_ref):
    u_mla = _dot(om_ref[...], wbm_ref[...])
    u_diff = _dot(od_ref[...], wbd_ref[...])
    g = gate_ref[...].astype(F32)
    mixed = (g[:, :D_MODEL] * u_mla + g[:, D_MODEL:] * u_diff).astype(BF16)
    y = _dot(mixed, wo_ref[...])
    x1 = x_ref[...] + _rms(y, gpa_ref[...])
    h = _rms(x1, gpm_ref[...]).astype(BF16)
    up = jnp.maximum(_dot(h, wup_ref[...]), 0.0)
    m = _dot((up * up).astype(BF16), wdn_ref[...])
    o_ref[...] = x1 + _rms(m, gpo_ref[...])


def _post_call(x2, om, od, gate, wbm, wbd, wo, gpa, gpm, wup, wdn, gpo):
    t = x2.shape[0]
    tm = TM_POST
    row = lambda c: pl.BlockSpec((tm, c), lambda i: (i, 0))
    return pl.pallas_call(
        _post_kernel,
        grid=(t // tm,),
        in_specs=[row(D_MODEL), row(om.shape[1]), row(od.shape[1]), row(gate.shape[1]),
                  _const_spec(wbm.shape), _const_spec(wbd.shape), _const_spec(wo.shape),
                  _const_spec((1, D_MODEL)), _const_spec((1, D_MODEL)),
                  _const_spec(wup.shape), _const_spec(wdn.shape), _const_spec((1, D_MODEL))],
        out_specs=row(D_MODEL),
        out_shape=jax.ShapeDtypeStruct((t, D_MODEL), F32),
        compiler_params=pltpu.CompilerParams(
            dimension_semantics=("arbitrary",), vmem_limit_bytes=VMEM_LIMIT),
        name="post",
    )(x2, om, od, gate, wbm, wbd, wo, gpa, gpm, wup, wdn, gpo)


def _diff_perm():
    half = D_DIFF_HEAD // 2
    idx = np.zeros(C_DQ, np.int32)
    for j in range(N_DIFF_HEADS):
        for ln in range(LANES):
            head = 2 * j + (ln // half) % 2
            tdim = ln % half + half * (ln // (2 * half))
            idx[j * LANES + ln] = head * D_DIFF_HEAD + tdim
    return idx


def _rot_half_cols(w):
    d = w.shape[-1]
    return jnp.concatenate([-w[..., d // 2:], w[..., :d // 2]], axis=-1)


def _rope_tables(seq):
    pos = jnp.arange(seq, dtype=F32)

    def cs(d):
        inv = 1.0 / (ROPE_THETA ** (jnp.arange(0, d, 2, dtype=F32) / d))
        ang = pos[:, None] * inv[None, :]
        return jnp.cos(ang), jnp.sin(ang)

    cd, sd = cs(D_DIFF_HEAD)
    cosd = jnp.tile(cd, (1, 4))
    sind = jnp.concatenate([-sd, -sd, sd, sd], axis=1)
    cm, sm = cs(D_MLA_ROPE)
    one = jnp.ones((seq, D_MLA_NOPE), F32)
    zero_n = jnp.zeros((seq, D_MLA_NOPE), F32)
    zero_p = jnp.zeros((seq, LANES - D_MLA_NOPE - D_MLA_ROPE), F32)
    cosm = jnp.concatenate([one, cm, cm, zero_p], axis=1)
    sinm = jnp.concatenate([zero_n, sm, sm, zero_p], axis=1)
    return cosd, sind, cosm, sinm


def _layout_in_proj(w_in):
    d = w_in.shape[0]
    s0 = C_QA
    s1 = s0 + C_KVA
    s2 = s1 + C_DQ
    s3 = s2 + C_DK
    s4 = s3 + C_DV
    w_qa = w_in[:, :s0]
    w_ckv = w_in[:, s0:s0 + KV_LORA]
    w_kr = w_in[:, s0 + KV_LORA:s1]
    zn = jnp.zeros((d, D_MLA_NOPE), w_in.dtype)
    zp = jnp.zeros((d, LANES - D_MLA_NOPE - D_MLA_ROPE), w_in.dtype)
    g1 = jnp.concatenate([zn, w_kr, zp], axis=1)
    g2 = jnp.concatenate([zn, _rot_half_cols(w_kr), zp], axis=1)
    perm = _diff_perm()
    w_dq = jnp.take(w_in[:, s1:s2], perm, axis=1)
    w_dk = jnp.take(w_in[:, s2:s3], perm, axis=1)
    w_all = jnp.concatenate([w_qa, w_ckv, g1, g2, w_dq, w_dk, w_in[:, s3:s4], w_in[:, s4:]], axis=1)
    assert w_all.shape[1] == W_ALL_COLS
    return w_all.astype(BF16)


def _layout_q_b(w_q_b):
    r = w_q_b.shape[0]
    w = w_q_b.reshape(r, N_MLA_HEADS, D_MLA_NOPE + D_MLA_ROPE)
    nope, rp = w[..., :D_MLA_NOPE], w[..., D_MLA_NOPE:]
    zp = jnp.zeros((r, N_MLA_HEADS, LANES - D_MLA_NOPE - D_MLA_ROPE), w.dtype)
    zn = jnp.zeros((r, N_MLA_HEADS, D_MLA_NOPE), w.dtype)
    a = jnp.concatenate([nope, rp, zp], axis=-1).reshape(r, N_MLA_HEADS * LANES)
    b = jnp.concatenate([zn, _rot_half_cols(rp), zp], axis=-1).reshape(r, N_MLA_HEADS * LANES)
    return jnp.concatenate([a, b], axis=1).astype(BF16)


def _layout_kv_b(w_kv_b):
    r = w_kv_b.shape[0]
    w = w_kv_b.reshape(r, N_MLA_HEADS, D_MLA_NOPE + D_MLA_V)
    kn, v = w[..., :D_MLA_NOPE], w[..., D_MLA_NOPE:]
    zk = jnp.zeros((r, N_MLA_HEADS, LANES - D_MLA_NOPE), w.dtype)
    wk = jnp.concatenate([kn, zk], axis=-1).reshape(r, N_MLA_HEADS * LANES)
    wv = v.reshape(r, N_MLA_HEADS * D_MLA_V)
    return jnp.concatenate([wk, wv], axis=1).astype(BF16)


def kernel(x, pre_attn_g, w_in, q_norm_g, w_q_b, kv_norm_g, w_kv_b, lambda_q1, lambda_k1, lambda_q2, lambda_k2, subln_g, w_br_mla, w_br_diff, w_out, post_attn_g, pre_mlp_g, w_mlp_up, w_mlp_down, post_mlp_g):
    batch, seq, d_model = x.shape
    depth = w_in.shape[0]
    assert d_model == D_MODEL and seq % TQ == 0 and seq % TM_PROJ == 0
    assert (batch * seq) % TM_POST == 0
    cosd, sind, cosm, sinm = _rope_tables(seq)
    x2 = x.reshape(batch * seq, d_model)
    row = lambda v: v.reshape(1, -1).astype(F32)
    for l in range(depth):
        lam_init = 0.8 - 0.6 * math.exp(-0.3 * l)
        qm, km, vm, qd, kd, vd, gate = _proj_call(
            x2, row(pre_attn_g[l]), _layout_in_proj(w_in[l]), row(q_norm_g[l]),
            _layout_q_b(w_q_b[l]), row(kv_norm_g[l]), _layout_kv_b(w_kv_b[l]),
            cosd, sind, cosm, sinm, seq)
        om = _mla_attn_call(qm, km, vm, batch, seq)
        lam_p = jnp.stack([lambda_q1[l], lambda_k1[l], lambda_q2[l], lambda_k2[l]]).astype(F32)
        od = _diff_attn_call(lam_p, row(subln_g[l]), qd, kd, vd, batch, seq, lam_init)
        x2 = _post_call(
            x2, om, od, gate, w_br_mla[l].astype(BF16), w_br_diff[l].astype(BF16),
            w_out[l].astype(BF16), row(post_attn_g[l]), row(pre_mlp_g[l]),
            w_mlp_up[l].astype(BF16), w_mlp_down[l].astype(BF16), row(post_mlp_g[l]))
    return x2.reshape(batch, seq, d_model)
```

```python
import functools
import math

import numpy as np
import jax
import jax.numpy as jnp
from jax import lax
from jax.experimental import pallas as pl
from jax.experimental.pallas import tpu as pltpu

D_MODEL = 1024
N_MLA_HEADS = 8
D_MLA_NOPE = 64
D_MLA_ROPE = 32
D_MLA_V = 64
Q_LORA = 256
KV_LORA = 128
N_DIFF_HEADS = 8
D_DIFF_HEAD = 64
D_FF = 4 * D_MODEL
ROPE_THETA = 10000.0
EPS = 1e-6

LANES = 128
VMEM_LIMIT = 52 * 1024 * 1024

C_QA = Q_LORA
C_KVA = KV_LORA + D_MLA_ROPE
C_DQ = 2 * N_DIFF_HEADS * D_DIFF_HEAD
C_DK = C_DQ
C_DV = N_DIFF_HEADS * 2 * D_DIFF_HEAD
C_GATE = 2 * D_MODEL

O_QA = 0
O_CKV = O_QA + Q_LORA
O_KR = O_CKV + KV_LORA
O_DQ = O_KR + 2 * LANES
O_DK = O_DQ + C_DQ
O_DV = O_DK + C_DK
O_GATE = O_DV + C_DV
W_ALL_COLS = O_GATE + C_GATE

TM_PROJ = 256
TM_POST = 256
MLA_TQ, MLA_TK = 512, 256
DIFF_TQ, DIFF_TK = 512, 512

LOG2E = math.log2(math.e)

F32 = jnp.float32
BF16 = jnp.bfloat16


def _rms(x, g):
    return x * lax.rsqrt(jnp.mean(x * x, axis=-1, keepdims=True) + EPS) * g


def _dot(a, b):
    return jnp.dot(a, b, preferred_element_type=F32)


def _dot_nt(a, b):
    return lax.dot_general(a, b, (((1,), (1,)), ((), ())), preferred_element_type=F32)


def _proj_kernel(x_ref, g_ref, w_ref, qng_ref, wqb_ref, kvg_ref, wkvb_ref,
                 cosd_ref, sind_ref, cosm_ref, sinm_ref,
                 qm_ref, km_ref, vm_ref, qd_ref, kd_ref, vd_ref, gate_ref,
                 *, mla_scale, diff_scale):
    h = _rms(x_ref[...], g_ref[...]).astype(BF16)

    def seg(a, b):
        return _dot(h, w_ref[:, a:b])

    cosm = cosm_ref[...]
    sinm = sinm_ref[...]

    qn = _rms(seg(O_QA, O_QA + Q_LORA), qng_ref[...]).astype(BF16)
    qab = _dot(qn, wqb_ref[...])
    half = N_MLA_HEADS * LANES
    for hh in range(N_MLA_HEADS):
        a = qab[:, hh * LANES:(hh + 1) * LANES]
        b = qab[:, half + hh * LANES:half + (hh + 1) * LANES]
        qm_ref[:, hh * LANES:(hh + 1) * LANES] = ((a * cosm + b * sinm) * mla_scale).astype(BF16)

    kn = _rms(seg(O_CKV, O_CKV + KV_LORA), kvg_ref[...]).astype(BF16)
    kvb = _dot(kn, wkvb_ref[...])
    kr = seg(O_KR, O_KR + 2 * LANES)
    krp = kr[:, :LANES] * cosm + kr[:, LANES:] * sinm
    for hh in range(N_MLA_HEADS):
        km_ref[:, hh * LANES:(hh + 1) * LANES] = (kvb[:, hh * LANES:(hh + 1) * LANES] + krp).astype(BF16)
    vm_ref[...] = kvb[:, half:].astype(BF16)

    cosd = cosd_ref[...]
    sind = sind_ref[...]
    lane = lax.broadcasted_iota(jnp.int32, (1, LANES), 1)
    is_a = (lane // (D_DIFF_HEAD // 2)) % 2 == 0
    mask_a = jnp.where(is_a, diff_scale, 0.0).astype(F32)
    mask_b = jnp.where(is_a, 0.0, diff_scale).astype(F32)

    def rope_group(xg):
        return xg * cosd + pltpu.roll(xg, LANES // 2, 1) * sind

    dq = seg(O_DQ, O_DQ + C_DQ)
    for j in range(N_DIFF_HEADS):
        r = rope_group(dq[:, j * LANES:(j + 1) * LANES])
        qd_ref[:, 2 * j * LANES:(2 * j + 1) * LANES] = (r * mask_a).astype(BF16)
        qd_ref[:, (2 * j + 1) * LANES:(2 * j + 2) * LANES] = (r * mask_b).astype(BF16)
    dk = seg(O_DK, O_DK + C_DK)
    for j in range(N_DIFF_HEADS):
        kd_ref[:, j * LANES:(j + 1) * LANES] = rope_group(dk[:, j * LANES:(j + 1) * LANES]).astype(BF16)

    vd_ref[...] = seg(O_DV, O_DV + C_DV).astype(BF16)
    gl = seg(O_GATE, O_GATE + C_GATE)
    gate_ref[...] = (1.0 / (1.0 + jnp.exp(-gl))).astype(BF16)


def _const_spec(shape):
    return pl.BlockSpec(shape, lambda i: (0,) * len(shape), pipeline_mode=pl.Buffered(1))


def _proj_call(x2, g, w_all, qng, wqb, kvg, wkvb, cosd, sind, cosm, sinm, seq):
    t = x2.shape[0]
    tm = TM_PROJ
    tiles_per_seq = seq // tm
    row = lambda c: pl.BlockSpec((tm, c), lambda i: (i, 0))
    tab = pl.BlockSpec((tm, LANES), lambda i: (i % tiles_per_seq, 0))
    widths = (N_MLA_HEADS * LANES, N_MLA_HEADS * LANES, N_MLA_HEADS * D_MLA_V,
              2 * C_DQ, C_DK, C_DV, C_GATE)
    kern = functools.partial(
        _proj_kernel,
        mla_scale=LOG2E / math.sqrt(D_MLA_NOPE + D_MLA_ROPE),
        diff_scale=LOG2E / math.sqrt(D_DIFF_HEAD))
    return pl.pallas_call(
        kern,
        grid=(t // tm,),
        in_specs=[row(D_MODEL), _const_spec((1, D_MODEL)), _const_spec(w_all.shape),
                  _const_spec((1, Q_LORA)), _const_spec(wqb.shape),
                  _const_spec((1, KV_LORA)), _const_spec(wkvb.shape),
                  tab, tab, tab, tab],
        out_specs=[row(c) for c in widths],
        out_shape=[jax.ShapeDtypeStruct((t, c), BF16) for c in widths],
        compiler_params=pltpu.CompilerParams(
            dimension_semantics=("arbitrary",), vmem_limit_bytes=VMEM_LIMIT),
        name="proj",
    )(x2, g, w_all, qng, wqb, kvg, wkvb, cosd, sind, cosm, sinm)


ONES_ROWS = 16


def _transpose_bf16(x):
    return x.astype(F32).T.astype(BF16)


def _causal_attention_t(q_ref, k_los, k_ref, vts, finish, tq, tk):
    n = len(k_los)
    seq = q_ref.shape[0]
    r = lax.broadcasted_iota(jnp.int32, (tk, tq), 0)
    c = lax.broadcasted_iota(jnp.int32, (tk, tq), 1)
    steps = [(qi, kb) for qi in range(seq // tq) for kb in range((qi + 1) * tq // tk)]
    qts = {}

    def scores(qi, kb):
        q0, st = qi * tq, kb * tk
        if kb == 0:
            qts[qi] = [_transpose_bf16(q_ref[q0:q0 + tq, i * LANES:(i + 1) * LANES]) for i in range(n)]
        out = []
        for i in range(n):
            s = _dot(k_ref[st:st + tk, k_los[i]:k_los[i] + LANES], qts[qi][i])
            if st + tk - 1 > q0:
                s = jnp.where(r + (st - q0) <= c, s, -jnp.inf)
            out.append(s)
        return out

    state = None
    s_next = scores(*steps[0])
    for t, (qi, kb) in enumerate(steps):
        s_cur = s_next
        if t + 1 < len(steps):
            s_next = scores(*steps[t + 1])
        st = kb * tk
        new_state = []
        for i in range(n):
            s = s_cur[i]
            m_blk = jnp.max(s, axis=0, keepdims=True)
            if kb == 0:
                m_new = m_blk
                acc = _dot(vts[i][:, st:st + tk], jnp.exp2(s - m_new).astype(BF16))
            else:
                m, acc = state[i]
                m_new = jnp.maximum(m, m_blk)
                p = jnp.exp2(s - m_new).astype(BF16)
                acc = jnp.exp2(m - m_new) * acc + _dot(vts[i][:, st:st + tk], p)
            new_state.append((m_new, acc))
        state = new_state
        if kb == (qi + 1) * tq // tk - 1:
            finish(qi * tq, [a for _, a in state])


def _mla_attn_kernel(q_ref, k_ref, v_ref, o_ref, vt_ref):
    dv = D_MLA_V
    seq = q_ref.shape[0]
    vt = v_ref[...].astype(F32).T
    ones = jnp.ones((ONES_ROWS, seq), BF16)
    for hh in range(2):
        vt_ref[hh, :dv, :] = vt[hh * dv:(hh + 1) * dv].astype(BF16)
        vt_ref[hh, dv:, :] = ones

    def finish(q0, accs):
        ot = jnp.concatenate([a[:dv] / a[dv:dv + 1] for a in accs], axis=0)
        o_ref[q0:q0 + MLA_TQ, :] = ot.T.astype(BF16)

    _causal_attention_t(q_ref, [0, LANES], k_ref, [vt_ref.at[0], vt_ref.at[1]], finish, MLA_TQ, MLA_TK)


def _diff_attn_kernel(lam_ref, sg_ref, q_ref, k_ref, v_ref, o_ref, vt_ref, *, lam_init):
    dv = 2 * D_DIFF_HEAD
    seq = q_ref.shape[0]
    vt_ref[:dv, :] = _transpose_bf16(v_ref[...])
    vt_ref[dv:, :] = jnp.ones((ONES_ROWS, seq), BF16)
    lp = lam_ref[...]
    lam = (jnp.exp(jnp.sum(lp[0:1] * lp[1:2], axis=-1, keepdims=True))
           - jnp.exp(jnp.sum(lp[2:3] * lp[3:4], axis=-1, keepdims=True)) + lam_init)
    gain = sg_ref[...] * (1.0 - lam_init)

    def finish(q0, accs):
        a1, a2 = accs
        o = a1[:dv] / a1[dv:dv + 1] - lam * (a2[:dv] / a2[dv:dv + 1])
        on = o * lax.rsqrt(jnp.mean(o * o, axis=0, keepdims=True) + EPS)
        o_ref[q0:q0 + DIFF_TQ, :] = (on.T * gain).astype(BF16)

    _causal_attention_t(q_ref, [0, 0], k_ref, [vt_ref, vt_ref], finish, DIFF_TQ, DIFF_TK)


def _attn_params():
    return pltpu.CompilerParams(
        dimension_semantics=("arbitrary", "arbitrary"), vmem_limit_bytes=VMEM_LIMIT)


def _mla_attn_call(qm, km, vm, batch, seq):
    t = qm.shape[0]
    pairs = N_MLA_HEADS // 2
    return pl.pallas_call(
        _mla_attn_kernel,
        grid=(batch, pairs),
        in_specs=[pl.BlockSpec((seq, 2 * LANES), lambda b, p: (b, p)),
                  pl.BlockSpec((seq, 2 * LANES), lambda b, p: (b, p)),
                  pl.BlockSpec((seq, LANES), lambda b, p: (b, p))],
        out_specs=pl.BlockSpec((seq, LANES), lambda b, p: (b, p)),
        out_shape=jax.ShapeDtypeStruct((t, N_MLA_HEADS * D_MLA_V), BF16),
        scratch_shapes=[pltpu.VMEM((2, D_MLA_V + ONES_ROWS, seq), BF16)],
        compiler_params=_attn_params(),
        name="mla_attn",
    )(qm, km, vm)


def _diff_attn_call(lam_p, sg, qd, kd, vd, batch, seq, lam_init):
    t = qd.shape[0]
    return pl.pallas_call(
        functools.partial(_diff_attn_kernel, lam_init=lam_init),
        grid=(batch, N_DIFF_HEADS),
        in_specs=[pl.BlockSpec(lam_p.shape, lambda b, j: (0, 0)),
                  pl.BlockSpec(sg.shape, lambda b, j: (0, 0)),
                  pl.BlockSpec((seq, 2 * LANES), lambda b, j: (b, j)),
                  pl.BlockSpec((seq, LANES), lambda b, j: (b, j)),
                  pl.BlockSpec((seq, LANES), lambda b, j: (b, j))],
        out_specs=pl.BlockSpec((seq, LANES), lambda b, j: (b, j)),
        out_shape=jax.ShapeDtypeStruct((t, C_DV), BF16),
        scratch_shapes=[pltpu.VMEM((2 * D_DIFF_HEAD + ONES_ROWS, seq), BF16)],
        compiler_params=_attn_params(),
        name="diff_attn",
    )(lam_p, sg, qd, kd, vd)


def _post_kernel(x_ref, om_ref, od_ref, gate_ref, wbm_ref, wbd_ref, wo_ref, gpa_ref, gpm_ref,
                 wup_ref, wdn_ref, gpo_ref, o_ref):
    u_mla = _dot(om_ref[...], wbm_ref[...])
    u_diff = _dot(od_ref[...], wbd_ref[...])
    g = gate_ref[...].astype(F32)
    mixed = (g[:, :D_MODEL] * u_mla + g[:, D_MODEL:] * u_diff).astype(BF16)
    y = _dot(mixed, wo_ref[...])
    x1 = x_ref[...] + _rms(y, gpa_ref[...])
    h = _rms(x1, gpm_ref[...]).astype(BF16)
    up = jnp.maximum(_dot(h, wup_ref[...]), 0.0)
    m = _dot((up * up).astype(BF16), wdn_ref[...])
    o_ref[...] = x1 + _rms(m, gpo_ref[...])


def _post_call(x2, om, od, gate, wbm, wbd, wo, gpa, gpm, wup, wdn, gpo):
    t = x2.shape[0]
    tm = TM_POST
    row = lambda c: pl.BlockSpec((tm, c), lambda i: (i, 0))
    return pl.pallas_call(
        _post_kernel,
        grid=(t // tm,),
        in_specs=[row(D_MODEL), row(om.shape[1]), row(od.shape[1]), row(gate.shape[1]),
                  _const_spec(wbm.shape), _const_spec(wbd.shape), _const_spec(wo.shape),
                  _const_spec((1, D_MODEL)), _const_spec((1, D_MODEL)),
                  _const_spec(wup.shape), _const_spec(wdn.shape), _const_spec((1, D_MODEL))],
        out_specs=row(D_MODEL),
        out_shape=jax.ShapeDtypeStruct((t, D_MODEL), F32),
        compiler_params=pltpu.CompilerParams(
            dimension_semantics=("arbitrary",), vmem_limit_bytes=VMEM_LIMIT),
        name="post",
    )(x2, om, od, gate, wbm, wbd, wo, gpa, gpm, wup, wdn, gpo)


def _diff_perm():
    half = D_DIFF_HEAD // 2
    idx = np.zeros(C_DQ, np.int32)
    for j in range(N_DIFF_HEADS):
        for ln in range(LANES):
            head = 2 * j + (ln // half) % 2
            tdim = ln % half + half * (ln // (2 * half))
            idx[j * LANES + ln] = head * D_DIFF_HEAD + tdim
    return idx


def _rot_half_cols(w):
    d = w.shape[-1]
    return jnp.concatenate([-w[..., d // 2:], w[..., :d // 2]], axis=-1)


def _rope_tables(seq):
    pos = jnp.arange(seq, dtype=F32)

    def cs(d):
        inv = 1.0 / (ROPE_THETA ** (jnp.arange(0, d, 2, dtype=F32) / d))
        ang = pos[:, None] * inv[None, :]
        return jnp.cos(ang), jnp.sin(ang)

    cd, sd = cs(D_DIFF_HEAD)
    cosd = jnp.tile(cd, (1, 4))
    sind = jnp.concatenate([-sd, -sd, sd, sd], axis=1)
    cm, sm = cs(D_MLA_ROPE)
    one = jnp.ones((seq, D_MLA_NOPE), F32)
    zero_n = jnp.zeros((seq, D_MLA_NOPE), F32)
    zero_p = jnp.zeros((seq, LANES - D_MLA_NOPE - D_MLA_ROPE), F32)
    cosm = jnp.concatenate([one, cm, cm, zero_p], axis=1)
    sinm = jnp.concatenate([zero_n, sm, sm, zero_p], axis=1)
    return cosd, sind, cosm, sinm


def _layout_in_proj(w_in):
    d = w_in.shape[0]
    s0 = C_QA
    s1 = s0 + C_KVA
    s2 = s1 + C_DQ
    s3 = s2 + C_DK
    s4 = s3 + C_DV
    w_qa = w_in[:, :s0]
    w_ckv = w_in[:, s0:s0 + KV_LORA]
    w_kr = w_in[:, s0 + KV_LORA:s1]
    zn = jnp.zeros((d, D_MLA_NOPE), w_in.dtype)
    zp = jnp.zeros((d, LANES - D_MLA_NOPE - D_MLA_ROPE), w_in.dtype)
    g1 = jnp.concatenate([zn, w_kr, zp], axis=1)
    g2 = jnp.concatenate([zn, _rot_half_cols(w_kr), zp], axis=1)
    perm = _diff_perm()
    w_dq = jnp.take(w_in[:, s1:s2], perm, axis=1)
    w_dk = jnp.take(w_in[:, s2:s3], perm, axis=1)
    w_all = jnp.concatenate([w_qa, w_ckv, g1, g2, w_dq, w_dk, w_in[:, s3:s4], w_in[:, s4:]], axis=1)
    assert w_all.shape[1] == W_ALL_COLS
    return w_all.astype(BF16)


def _layout_q_b(w_q_b):
    r = w_q_b.shape[0]
    w = w_q_b.reshape(r, N_MLA_HEADS, D_MLA_NOPE + D_MLA_ROPE)
    nope, rp = w[..., :D_MLA_NOPE], w[..., D_MLA_NOPE:]
    zp = jnp.zeros((r, N_MLA_HEADS, LANES - D_MLA_NOPE - D_MLA_ROPE), w.dtype)
    zn = jnp.zeros((r, N_MLA_HEADS, D_MLA_NOPE), w.dtype)
    a = jnp.concatenate([nope, rp, zp], axis=-1).reshape(r, N_MLA_HEADS * LANES)
    b = jnp.concatenate([zn, _rot_half_cols(rp), zp], axis=-1).reshape(r, N_MLA_HEADS * LANES)
    return jnp.concatenate([a, b], axis=1).astype(BF16)


def _layout_kv_b(w_kv_b):
    r = w_kv_b.shape[0]
    w = w_kv_b.reshape(r, N_MLA_HEADS, D_MLA_NOPE + D_MLA_V)
    kn, v = w[..., :D_MLA_NOPE], w[..., D_MLA_NOPE:]
    zk = jnp.zeros((r, N_MLA_HEADS, LANES - D_MLA_NOPE), w.dtype)
    wk = jnp.concatenate([kn, zk], axis=-1).reshape(r, N_MLA_HEADS * LANES)
    wv = v.reshape(r, N_MLA_HEADS * D_MLA_V)
    return jnp.concatenate([wk, wv], axis=1).astype(BF16)


def kernel(x, pre_attn_g, w_in, q_norm_g, w_q_b, kv_norm_g, w_kv_b, lambda_q1, lambda_k1, lambda_q2, lambda_k2, subln_g, w_br_mla, w_br_diff, w_out, post_attn_g, pre_mlp_g, w_mlp_up, w_mlp_down, post_mlp_g):
    batch, seq, d_model = x.shape
    depth = w_in.shape[0]
    assert d_model == D_MODEL and seq % MLA_TQ == 0 and seq % DIFF_TQ == 0 and seq % TM_PROJ == 0
    assert MLA_TQ % MLA_TK == 0 and DIFF_TQ % DIFF_TK == 0
    assert (batch * seq) % TM_POST == 0
    cosd, sind, cosm, sinm = _rope_tables(seq)
    x2 = x.reshape(batch * seq, d_model)
    row = lambda v: v.reshape(1, -1).astype(F32)
    for l in range(depth):
        lam_init = 0.8 - 0.6 * math.exp(-0.3 * l)
        qm, km, vm, qd, kd, vd, gate = _proj_call(
            x2, row(pre_attn_g[l]), _layout_in_proj(w_in[l]), row(q_norm_g[l]),
            _layout_q_b(w_q_b[l]), row(kv_norm_g[l]), _layout_kv_b(w_kv_b[l]),
            cosd, sind, cosm, sinm, seq)
        om = _mla_attn_call(qm, km, vm, batch, seq)
        lam_p = jnp.stack([lambda_q1[l], lambda_k1[l], lambda_q2[l], lambda_k2[l]]).astype(F32)
        od = _diff_attn_call(lam_p, row(subln_g[l]), qd, kd, vd, batch, seq, lam_init)
        x2 = _post_call(
            x2, om, od, gate, w_br_mla[l].astype(BF16), w_br_diff[l].astype(BF16),
            w_out[l].astype(BF16), row(post_attn_g[l]), row(pre_mlp_g[l]),
            w_mlp_up[l].astype(BF16), w_mlp_down[l].astype(BF16), row(post_mlp_g[l]))
    return x2.reshape(batch, seq, d_model)
```

```python
import functools
import math

import numpy as np
import jax
import jax.numpy as jnp
from jax import lax
from jax.experimental import pallas as pl
from jax.experimental.pallas import tpu as pltpu

D_MODEL = 1024
N_MLA_HEADS = 8
D_MLA_NOPE = 64
D_MLA_ROPE = 32
D_MLA_V = 64
Q_LORA = 256
KV_LORA = 128
N_DIFF_HEADS = 8
D_DIFF_HEAD = 64
D_FF = 4 * D_MODEL
ROPE_THETA = 10000.0
EPS = 1e-6

LANES = 128
VMEM_LIMIT = 52 * 1024 * 1024

C_QA = Q_LORA
C_KVA = KV_LORA + D_MLA_ROPE
C_DQ = 2 * N_DIFF_HEADS * D_DIFF_HEAD
C_DK = C_DQ
C_DV = N_DIFF_HEADS * 2 * D_DIFF_HEAD
C_GATE = 2 * D_MODEL

O_QA = 0
O_CKV = O_QA + Q_LORA
O_KR = O_CKV + KV_LORA
O_DQ = O_KR + 2 * LANES
O_DK = O_DQ + C_DQ
O_DV = O_DK + C_DK
O_GATE = O_DV + C_DV
W_ALL_COLS = O_GATE + C_GATE

TM_PROJ = 256
TM_POST = 256
MLA_TQ, MLA_TK, MLA_LOOKAHEAD = 512, 256, 3
DIFF_TQ, DIFF_TK, DIFF_LOOKAHEAD = 1024, 512, 2
DIAG_TK = 256

LOG2E = math.log2(math.e)

F32 = jnp.float32
BF16 = jnp.bfloat16


def _rms(x, g):
    return x * lax.rsqrt(jnp.mean(x * x, axis=-1, keepdims=True) + EPS) * g


def _dot(a, b):
    return jnp.dot(a, b, preferred_element_type=F32)


def _dot_nt(a, b):
    return lax.dot_general(a, b, (((1,), (1,)), ((), ())), preferred_element_type=F32)


def _proj_kernel(x_ref, g_ref, w_ref, qng_ref, wqb_ref, kvg_ref, wkvb_ref,
                 cosd_ref, sind_ref, cosm_ref, sinm_ref,
                 qm_ref, km_ref, vm_ref, qd_ref, kd_ref, vd_ref, gate_ref,
                 *, mla_scale, diff_scale):
    h = _rms(x_ref[...], g_ref[...]).astype(BF16)

    def seg(a, b):
        return _dot(h, w_ref[:, a:b])

    cosm = cosm_ref[...]
    sinm = sinm_ref[...]

    qn = _rms(seg(O_QA, O_QA + Q_LORA), qng_ref[...]).astype(BF16)
    qab = _dot(qn, wqb_ref[...])
    half = N_MLA_HEADS * LANES
    for hh in range(N_MLA_HEADS):
        a = qab[:, hh * LANES:(hh + 1) * LANES]
        b = qab[:, half + hh * LANES:half + (hh + 1) * LANES]
        qm_ref[:, hh * LANES:(hh + 1) * LANES] = ((a * cosm + b * sinm) * mla_scale).astype(BF16)

    kn = _rms(seg(O_CKV, O_CKV + KV_LORA), kvg_ref[...]).astype(BF16)
    kvb = _dot(kn, wkvb_ref[...])
    kr = seg(O_KR, O_KR + 2 * LANES)
    krp = kr[:, :LANES] * cosm + kr[:, LANES:] * sinm
    for hh in range(N_MLA_HEADS):
        km_ref[:, hh * LANES:(hh + 1) * LANES] = (kvb[:, hh * LANES:(hh + 1) * LANES] + krp).astype(BF16)
    vm_ref[...] = kvb[:, half:].astype(BF16)

    cosd = cosd_ref[...]
    sind = sind_ref[...]
    lane = lax.broadcasted_iota(jnp.int32, (1, LANES), 1)
    is_a = (lane // (D_DIFF_HEAD // 2)) % 2 == 0
    mask_a = jnp.where(is_a, diff_scale, 0.0).astype(F32)
    mask_b = jnp.where(is_a, 0.0, diff_scale).astype(F32)

    def rope_group(xg):
        return xg * cosd + pltpu.roll(xg, LANES // 2, 1) * sind

    dq = seg(O_DQ, O_DQ + C_DQ)
    for j in range(N_DIFF_HEADS):
        r = rope_group(dq[:, j * LANES:(j + 1) * LANES])
        qd_ref[:, 2 * j * LANES:(2 * j + 1) * LANES] = (r * mask_a).astype(BF16)
        qd_ref[:, (2 * j + 1) * LANES:(2 * j + 2) * LANES] = (r * mask_b).astype(BF16)
    dk = seg(O_DK, O_DK + C_DK)
    for j in range(N_DIFF_HEADS):
        kd_ref[:, j * LANES:(j + 1) * LANES] = rope_group(dk[:, j * LANES:(j + 1) * LANES]).astype(BF16)

    vd_ref[...] = seg(O_DV, O_DV + C_DV).astype(BF16)
    gl = seg(O_GATE, O_GATE + C_GATE)
    gate_ref[...] = (1.0 / (1.0 + jnp.exp(-gl))).astype(BF16)


def _const_spec(shape):
    return pl.BlockSpec(shape, lambda i: (0,) * len(shape), pipeline_mode=pl.Buffered(1))


def _proj_call(x2, g, w_all, qng, wqb, kvg, wkvb, cosd, sind, cosm, sinm, seq):
    t = x2.shape[0]
    tm = TM_PROJ
    tiles_per_seq = seq // tm
    row = lambda c: pl.BlockSpec((tm, c), lambda i: (i, 0))
    tab = pl.BlockSpec((tm, LANES), lambda i: (i % tiles_per_seq, 0))
    widths = (N_MLA_HEADS * LANES, N_MLA_HEADS * LANES, N_MLA_HEADS * D_MLA_V,
              2 * C_DQ, C_DK, C_DV, C_GATE)
    kern = functools.partial(
        _proj_kernel,
        mla_scale=LOG2E / math.sqrt(D_MLA_NOPE + D_MLA_ROPE),
        diff_scale=LOG2E / math.sqrt(D_DIFF_HEAD))
    return pl.pallas_call(
        kern,
        grid=(t // tm,),
        in_specs=[row(D_MODEL), _const_spec((1, D_MODEL)), _const_spec(w_all.shape),
                  _const_spec((1, Q_LORA)), _const_spec(wqb.shape),
                  _const_spec((1, KV_LORA)), _const_spec(wkvb.shape),
                  tab, tab, tab, tab],
        out_specs=[row(c) for c in widths],
        out_shape=[jax.ShapeDtypeStruct((t, c), BF16) for c in widths],
        compiler_params=pltpu.CompilerParams(
            dimension_semantics=("arbitrary",), vmem_limit_bytes=VMEM_LIMIT),
        name="proj",
    )(x2, g, w_all, qng, wqb, kvg, wkvb, cosd, sind, cosm, sinm)


ONES_ROWS = 16


def _transpose_bf16(x):
    return x.astype(F32).T.astype(BF16)


def _causal_attention_t(q_ref, k_los, k_ref, vts, finish, tq, tk, lookahead):
    n = len(k_los)
    seq = q_ref.shape[0]
    steps = []
    for qi in range(seq // tq):
        q0 = qi * tq
        blocks = [(kb * tk, tk, 0, False) for kb in range(q0 // tk)]
        blocks += [(q0 + d * DIAG_TK, DIAG_TK, d * DIAG_TK, True) for d in range(tq // DIAG_TK)]
        steps += [(q0,) + b + (j == 0, j == len(blocks) - 1) for j, b in enumerate(blocks)]
    units = [(step, i) for step in steps for i in range(n)]
    qts = {}

    def scores(step, i):
        q0, st, klen, c_lo, masked, first, _ = step
        if first and i == 0:
            qts[q0] = [_transpose_bf16(q_ref[q0:q0 + tq, j * LANES:(j + 1) * LANES]) for j in range(n)]
        s = _dot(k_ref[st:st + klen, k_los[i]:k_los[i] + LANES], qts[q0][i][:, c_lo:])
        if masked:
            r = lax.broadcasted_iota(jnp.int32, s.shape, 0)
            c = lax.broadcasted_iota(jnp.int32, s.shape, 1)
            s = jnp.where(r <= c, s, -jnp.inf)
        return s

    state = [None] * n
    pending = [scores(*u) for u in units[:lookahead]]
    for t, (step, i) in enumerate(units):
        q0, st, klen, c_lo, masked, first, last = step
        s = pending.pop(0)
        if t + lookahead < len(units):
            pending.append(scores(*units[t + lookahead]))
        vt = vts[i][:, st:st + klen]
        m_blk = jnp.max(s, axis=0, keepdims=True)
        if first:
            m_new = m_blk
            acc = _dot(vt, jnp.exp2(s - m_new).astype(BF16))
        else:
            m, acc = state[i]
            m_new = jnp.maximum(m[:, c_lo:], m_blk)
            p = jnp.exp2(s - m_new).astype(BF16)
            acc_new = jnp.exp2(m[:, c_lo:] - m_new) * acc[:, c_lo:] + _dot(vt, p)
            if c_lo:
                m_new = jnp.concatenate([m[:, :c_lo], m_new], axis=1)
                acc_new = jnp.concatenate([acc[:, :c_lo], acc_new], axis=1)
            acc = acc_new
        state[i] = (m_new, acc)
        if last and i == n - 1:
            finish(q0, [a for _, a in state])


def _mla_attn_kernel(q_ref, k_ref, v_ref, o_ref, vt_ref):
    dv = D_MLA_V
    seq = q_ref.shape[0]
    vt = v_ref[...].astype(F32).T
    ones = jnp.ones((ONES_ROWS, seq), BF16)
    for hh in range(2):
        vt_ref[hh, :dv, :] = vt[hh * dv:(hh + 1) * dv].astype(BF16)
        vt_ref[hh, dv:, :] = ones

    def finish(q0, accs):
        ot = jnp.concatenate([a[:dv] / a[dv:dv + 1] for a in accs], axis=0)
        o_ref[q0:q0 + MLA_TQ, :] = ot.T.astype(BF16)

    _causal_attention_t(q_ref, [0, LANES], k_ref, [vt_ref.at[0], vt_ref.at[1]], finish, MLA_TQ, MLA_TK, MLA_LOOKAHEAD)


def _diff_attn_kernel(lam_ref, sg_ref, q_ref, k_ref, v_ref, o_ref, vt_ref, *, lam_init):
    dv = 2 * D_DIFF_HEAD
    seq = q_ref.shape[0]
    vt_ref[:dv, :] = _transpose_bf16(v_ref[...])
    vt_ref[dv:, :] = jnp.ones((ONES_ROWS, seq), BF16)
    lp = lam_ref[...]
    lam = (jnp.exp(jnp.sum(lp[0:1] * lp[1:2], axis=-1, keepdims=True))
           - jnp.exp(jnp.sum(lp[2:3] * lp[3:4], axis=-1, keepdims=True)) + lam_init)
    gain = sg_ref[...] * (1.0 - lam_init)

    def finish(q0, accs):
        a1, a2 = accs
        o = a1[:dv] / a1[dv:dv + 1] - lam * (a2[:dv] / a2[dv:dv + 1])
        on = o * lax.rsqrt(jnp.mean(o * o, axis=0, keepdims=True) + EPS)
        o_ref[q0:q0 + DIFF_TQ, :] = (on.T * gain).astype(BF16)

    _causal_attention_t(q_ref, [0, 0], k_ref, [vt_ref, vt_ref], finish, DIFF_TQ, DIFF_TK, DIFF_LOOKAHEAD)


def _attn_params():
    return pltpu.CompilerParams(
        dimension_semantics=("arbitrary", "arbitrary"), vmem_limit_bytes=VMEM_LIMIT)


def _mla_attn_call(qm, km, vm, batch, seq):
    t = qm.shape[0]
    pairs = N_MLA_HEADS // 2
    return pl.pallas_call(
        _mla_attn_kernel,
        grid=(batch, pairs),
        in_specs=[pl.BlockSpec((seq, 2 * LANES), lambda b, p: (b, p)),
                  pl.BlockSpec((seq, 2 * LANES), lambda b, p: (b, p)),
                  pl.BlockSpec((seq, LANES), lambda b, p: (b, p))],
        out_specs=pl.BlockSpec((seq, LANES), lambda b, p: (b, p)),
        out_shape=jax.ShapeDtypeStruct((t, N_MLA_HEADS * D_MLA_V), BF16),
        scratch_shapes=[pltpu.VMEM((2, D_MLA_V + ONES_ROWS, seq), BF16)],
        compiler_params=_attn_params(),
        name="mla_attn",
    )(qm, km, vm)


def _diff_attn_call(lam_p, sg, qd, kd, vd, batch, seq, lam_init):
    t = qd.shape[0]
    return pl.pallas_call(
        functools.partial(_diff_attn_kernel, lam_init=lam_init),
        grid=(batch, N_DIFF_HEADS),
        in_specs=[pl.BlockSpec(lam_p.shape, lambda b, j: (0, 0)),
                  pl.BlockSpec(sg.shape, lambda b, j: (0, 0)),
                  pl.BlockSpec((seq, 2 * LANES), lambda b, j: (b, j)),
                  pl.BlockSpec((seq, LANES), lambda b, j: (b, j)),
                  pl.BlockSpec((seq, LANES), lambda b, j: (b, j))],
        out_specs=pl.BlockSpec((seq, LANES), lambda b, j: (b, j)),
        out_shape=jax.ShapeDtypeStruct((t, C_DV), BF16),
        scratch_shapes=[pltpu.VMEM((2 * D_DIFF_HEAD + ONES_ROWS, seq), BF16)],
        compiler_params=_attn_params(),
        name="diff_attn",
    )(lam_p, sg, qd, kd, vd)


def _post_kernel(x_ref, om_ref, od_ref, gate_ref, wbm_ref, wbd_ref, wo_ref, gpa_ref, gpm_ref,
                 wup_ref, wdn_ref, gpo_ref, o_ref):
    u_mla = _dot(om_ref[...], wbm_ref[...])
    u_diff = _dot(od_ref[...], wbd_ref[...])
    g = gate_ref[...].astype(F32)
    mixed = (g[:, :D_MODEL] * u_mla + g[:, D_MODEL:] * u_diff).astype(BF16)
    y = _dot(mixed, wo_ref[...])
    x1 = x_ref[...] + _rms(y, gpa_ref[...])
    h = _rms(x1, gpm_ref[...]).astype(BF16)
    up = jnp.maximum(_dot(h, wup_ref[...]), 0.0)
    m = _dot((up * up).astype(BF16), wdn_ref[...])
    o_ref[...] = x1 + _rms(m, gpo_ref[...])


def _post_call(x2, om, od, gate, wbm, wbd, wo, gpa, gpm, wup, wdn, gpo):
    t = x2.shape[0]
    tm = TM_POST
    row = lambda c: pl.BlockSpec((tm, c), lambda i: (i, 0))
    return pl.pallas_call(
        _post_kernel,
        grid=(t // tm,),
        in_specs=[row(D_MODEL), row(om.shape[1]), row(od.shape[1]), row(gate.shape[1]),
                  _const_spec(wbm.shape), _const_spec(wbd.shape), _const_spec(wo.shape),
                  _const_spec((1, D_MODEL)), _const_spec((1, D_MODEL)),
                  _const_spec(wup.shape), _const_spec(wdn.shape), _const_spec((1, D_MODEL))],
        out_specs=row(D_MODEL),
        out_shape=jax.ShapeDtypeStruct((t, D_MODEL), F32),
        compiler_params=pltpu.CompilerParams(
            dimension_semantics=("arbitrary",), vmem_limit_bytes=VMEM_LIMIT),
        name="post",
    )(x2, om, od, gate, wbm, wbd, wo, gpa, gpm, wup, wdn, gpo)


def _pair_interleave(w):
    d = w.shape[0]
    half = D_DIFF_HEAD // 2
    w = w.reshape(d, N_DIFF_HEADS, 2, 2, half)
    return w.transpose(0, 1, 3, 2, 4).reshape(d, N_DIFF_HEADS * LANES)


def _rot_half_cols(w):
    d = w.shape[-1]
    return jnp.concatenate([-w[..., d // 2:], w[..., :d // 2]], axis=-1)


def _rope_tables(seq):
    pos = np.arange(seq, dtype=np.float64)

    def cs(d):
        inv = 1.0 / (ROPE_THETA ** (np.arange(0, d, 2, dtype=np.float64) / d))
        ang = pos[:, None] * inv[None, :]
        return np.cos(ang), np.sin(ang)

    cd, sd = cs(D_DIFF_HEAD)
    cosd = np.tile(cd, (1, 4))
    sind = np.concatenate([-sd, -sd, sd, sd], axis=1)
    cm, sm = cs(D_MLA_ROPE)
    one = np.ones((seq, D_MLA_NOPE))
    zero_n = np.zeros((seq, D_MLA_NOPE))
    zero_p = np.zeros((seq, LANES - D_MLA_NOPE - D_MLA_ROPE))
    cosm = np.concatenate([one, cm, cm, zero_p], axis=1)
    sinm = np.concatenate([zero_n, sm, sm, zero_p], axis=1)
    return tuple(jnp.asarray(t, dtype=F32) for t in (cosd, sind, cosm, sinm))


def _layout_in_proj(w_in):
    w = w_in.astype(BF16)
    d = w.shape[0]
    s0 = C_QA
    s1 = s0 + C_KVA
    s2 = s1 + C_DQ
    s3 = s2 + C_DK
    w_kr = w[:, s0 + KV_LORA:s1]
    zn = jnp.zeros((d, D_MLA_NOPE), BF16)
    zp = jnp.zeros((d, LANES - D_MLA_NOPE - D_MLA_ROPE), BF16)
    w_all = jnp.concatenate(
        [w[:, :s0 + KV_LORA], zn, w_kr, zp, zn, _rot_half_cols(w_kr), zp,
         _pair_interleave(w[:, s1:s2]), _pair_interleave(w[:, s2:s3]), w[:, s3:]], axis=1)
    assert w_all.shape[1] == W_ALL_COLS
    return w_all


def _layout_q_b(w_q_b):
    r = w_q_b.shape[0]
    w = w_q_b.reshape(r, N_MLA_HEADS, D_MLA_NOPE + D_MLA_ROPE)
    nope, rp = w[..., :D_MLA_NOPE], w[..., D_MLA_NOPE:]
    zp = jnp.zeros((r, N_MLA_HEADS, LANES - D_MLA_NOPE - D_MLA_ROPE), w.dtype)
    zn = jnp.zeros((r, N_MLA_HEADS, D_MLA_NOPE), w.dtype)
    a = jnp.concatenate([nope, rp, zp], axis=-1).reshape(r, N_MLA_HEADS * LANES)
    b = jnp.concatenate([zn, _rot_half_cols(rp), zp], axis=-1).reshape(r, N_MLA_HEADS * LANES)
    return jnp.concatenate([a, b], axis=1).astype(BF16)


def _layout_kv_b(w_kv_b):
    r = w_kv_b.shape[0]
    w = w_kv_b.reshape(r, N_MLA_HEADS, D_MLA_NOPE + D_MLA_V)
    kn, v = w[..., :D_MLA_NOPE], w[..., D_MLA_NOPE:]
    zk = jnp.zeros((r, N_MLA_HEADS, LANES - D_MLA_NOPE), w.dtype)
    wk = jnp.concatenate([kn, zk], axis=-1).reshape(r, N_MLA_HEADS * LANES)
    wv = v.reshape(r, N_MLA_HEADS * D_MLA_V)
    return jnp.concatenate([wk, wv], axis=1).astype(BF16)


def kernel(x, pre_attn_g, w_in, q_norm_g, w_q_b, kv_norm_g, w_kv_b, lambda_q1, lambda_k1, lambda_q2, lambda_k2, subln_g, w_br_mla, w_br_diff, w_out, post_attn_g, pre_mlp_g, w_mlp_up, w_mlp_down, post_mlp_g):
    batch, seq, d_model = x.shape
    depth = w_in.shape[0]
    assert d_model == D_MODEL and seq % MLA_TQ == 0 and seq % DIFF_TQ == 0 and seq % TM_PROJ == 0
    assert MLA_TQ % MLA_TK == 0 and DIFF_TQ % DIFF_TK == 0 and MLA_TQ % DIAG_TK == 0 and DIFF_TQ % DIAG_TK == 0
    assert (batch * seq) % TM_POST == 0
    cosd, sind, cosm, sinm = _rope_tables(seq)
    x2 = x.reshape(batch * seq, d_model)
    row = lambda v: v.reshape(1, -1).astype(F32)
    for l in range(depth):
        lam_init = 0.8 - 0.6 * math.exp(-0.3 * l)
        qm, km, vm, qd, kd, vd, gate = _proj_call(
            x2, row(pre_attn_g[l]), _layout_in_proj(w_in[l]), row(q_norm_g[l]),
            _layout_q_b(w_q_b[l]), row(kv_norm_g[l]), _layout_kv_b(w_kv_b[l]),
            cosd, sind, cosm, sinm, seq)
        om = _mla_attn_call(qm, km, vm, batch, seq)
        lam_p = jnp.stack([lambda_q1[l], lambda_k1[l], lambda_q2[l], lambda_k2[l]]).astype(F32)
        od = _diff_attn_call(lam_p, row(subln_g[l]), qd, kd, vd, batch, seq, lam_init)
        x2 = _post_call(
            x2, om, od, gate, w_br_mla[l].astype(BF16), w_br_diff[l].astype(BF16),
            w_out[l].astype(BF16), row(post_attn_g[l]), row(pre_mlp_g[l]),
            w_mlp_up[l].astype(BF16), w_mlp_down[l].astype(BF16), row(post_mlp_g[l]))
    return x2.reshape(batch, seq, d_model)
```

```python
import functools
import math

import numpy as np
import jax
import jax.numpy as jnp
from jax import lax
from jax.experimental import pallas as pl
from jax.experimental.pallas import tpu as pltpu

D_MODEL = 1024
N_MLA_HEADS = 8
D_MLA_NOPE = 64
D_MLA_ROPE = 32
D_MLA_V = 64
Q_LORA = 256
KV_LORA = 128
N_DIFF_HEADS = 8
D_DIFF_HEAD = 64
D_FF = 4 * D_MODEL
ROPE_THETA = 10000.0
EPS = 1e-6

LANES = 128
VMEM_LIMIT = 52 * 1024 * 1024

C_QA = Q_LORA
C_KVA = KV_LORA + D_MLA_ROPE
C_DQ = 2 * N_DIFF_HEADS * D_DIFF_HEAD
C_DK = C_DQ
C_DV = N_DIFF_HEADS * 2 * D_DIFF_HEAD
C_GATE = 2 * D_MODEL

O_QA = 0
O_CKV = O_QA + Q_LORA
O_KR = O_CKV + KV_LORA
O_DQ = O_KR + 2 * LANES
O_DK = O_DQ + C_DQ
O_DV = O_DK + C_DK
O_GATE = O_DV + C_DV
W_ALL_COLS = O_GATE + C_GATE

TM_PROJ = 256
TM_POST = 512
SUB_POST = 256
FF_CHUNK = 1024
MLA_TQ, MLA_TK, MLA_LOOKAHEAD = 512, 256, 3
DIFF_TQ, DIFF_TK, DIFF_LOOKAHEAD = 1024, 512, 2
DIAG_TK = 256

LOG2E = math.log2(math.e)

F32 = jnp.float32
BF16 = jnp.bfloat16


def _rms(x, g):
    return x * lax.rsqrt(jnp.mean(x * x, axis=-1, keepdims=True) + EPS) * g


def _dot(a, b):
    return jnp.dot(a, b, preferred_element_type=F32)


def _dot_nt(a, b):
    return lax.dot_general(a, b, (((1,), (1,)), ((), ())), preferred_element_type=F32)


def _proj_kernel(x_ref, g_ref, w_ref, qng_ref, wqb_ref, kvg_ref, wkvb_ref,
                 cosd_ref, sind_ref, cosm_ref, sinm_ref,
                 qm_ref, km_ref, vm_ref, qd_ref, kd_ref, vd_ref, gate_ref,
                 *, mla_scale, diff_scale):
    h = _rms(x_ref[...], g_ref[...]).astype(BF16)

    def seg(a, b):
        return _dot(h, w_ref[:, a:b])

    qa = seg(O_QA, O_QA + Q_LORA)
    ckv = seg(O_CKV, O_CKV + KV_LORA)
    kr = seg(O_KR, O_KR + 2 * LANES)
    gl = seg(O_GATE, O_GATE + C_GATE)

    cosm = cosm_ref[...]
    sinm = sinm_ref[...]
    qn = _rms(qa, qng_ref[...]).astype(BF16)
    kn = _rms(ckv, kvg_ref[...]).astype(BF16)
    krp = kr[:, :LANES] * cosm + kr[:, LANES:] * sinm
    qab = _dot(qn, wqb_ref[...])
    kvb = _dot(kn, wkvb_ref[...])
    dq = seg(O_DQ, O_DQ + C_DQ)

    gate_ref[...] = (1.0 / (1.0 + jnp.exp(-gl))).astype(BF16)

    half = N_MLA_HEADS * LANES
    for hh in range(N_MLA_HEADS):
        a = qab[:, hh * LANES:(hh + 1) * LANES]
        b = qab[:, half + hh * LANES:half + (hh + 1) * LANES]
        qm_ref[:, hh * LANES:(hh + 1) * LANES] = ((a * cosm + b * sinm) * mla_scale).astype(BF16)
    for hh in range(N_MLA_HEADS):
        km_ref[:, hh * LANES:(hh + 1) * LANES] = (kvb[:, hh * LANES:(hh + 1) * LANES] + krp).astype(BF16)
    vm_ref[...] = kvb[:, half:].astype(BF16)

    dk = seg(O_DK, O_DK + C_DK)

    cosd = cosd_ref[...]
    sind = sind_ref[...]
    lane = lax.broadcasted_iota(jnp.int32, (1, LANES), 1)
    low_half = lane % D_DIFF_HEAD < D_DIFF_HEAD // 2
    mask_a = jnp.where(lane < D_DIFF_HEAD, diff_scale, 0.0).astype(F32)
    mask_b = jnp.where(lane < D_DIFF_HEAD, 0.0, diff_scale).astype(F32)

    def rope_group(xg):
        up = pltpu.roll(xg, LANES - D_DIFF_HEAD // 2, 1)
        down = pltpu.roll(xg, D_DIFF_HEAD // 2, 1)
        return xg * cosd + jnp.where(low_half, up, down) * sind

    for j in range(N_DIFF_HEADS):
        r = rope_group(dq[:, j * LANES:(j + 1) * LANES])
        qd_ref[:, 2 * j * LANES:(2 * j + 1) * LANES] = (r * mask_a).astype(BF16)
        qd_ref[:, (2 * j + 1) * LANES:(2 * j + 2) * LANES] = (r * mask_b).astype(BF16)

    dv = seg(O_DV, O_DV + C_DV)
    for j in range(N_DIFF_HEADS):
        kd_ref[:, j * LANES:(j + 1) * LANES] = rope_group(dk[:, j * LANES:(j + 1) * LANES]).astype(BF16)
    vd_ref[...] = dv.astype(BF16)


def _const_spec(shape):
    return pl.BlockSpec(shape, lambda i: (0,) * len(shape), pipeline_mode=pl.Buffered(1))


def _proj_call(x2, g, w_all, qng, wqb, kvg, wkvb, cosd, sind, cosm, sinm, seq):
    t = x2.shape[0]
    tm = TM_PROJ
    tiles_per_seq = seq // tm
    row = lambda c: pl.BlockSpec((tm, c), lambda i: (i, 0))
    tab = pl.BlockSpec((tm, LANES), lambda i: (i % tiles_per_seq, 0))
    widths = (N_MLA_HEADS * LANES, N_MLA_HEADS * LANES, N_MLA_HEADS * D_MLA_V,
              2 * C_DQ, C_DK, C_DV, C_GATE)
    kern = functools.partial(
        _proj_kernel,
        mla_scale=LOG2E / math.sqrt(D_MLA_NOPE + D_MLA_ROPE),
        diff_scale=LOG2E / math.sqrt(D_DIFF_HEAD))
    return pl.pallas_call(
        kern,
        grid=(t // tm,),
        in_specs=[row(D_MODEL), _const_spec((1, D_MODEL)), _const_spec(w_all.shape),
                  _const_spec((1, Q_LORA)), _const_spec(wqb.shape),
                  _const_spec((1, KV_LORA)), _const_spec(wkvb.shape),
                  tab, tab, tab, tab],
        out_specs=[row(c) for c in widths],
        out_shape=[jax.ShapeDtypeStruct((t, c), BF16) for c in widths],
        compiler_params=pltpu.CompilerParams(
            dimension_semantics=("arbitrary",), vmem_limit_bytes=VMEM_LIMIT),
        name="proj",
    )(x2, g, w_all, qng, wqb, kvg, wkvb, cosd, sind, cosm, sinm)


ONES_ROWS = 16


def _transpose_bf16(x):
    return x.astype(F32).T.astype(BF16)


def _causal_attention_t(q_ref, k_los, k_ref, vts, finish, tq, tk, lookahead):
    n = len(k_los)
    seq = q_ref.shape[0]
    steps = []
    for qi in range(seq // tq):
        q0 = qi * tq
        blocks = [(kb * tk, tk, 0, False) for kb in range(q0 // tk)]
        blocks += [(q0 + d * DIAG_TK, DIAG_TK, d * DIAG_TK, True) for d in range(tq // DIAG_TK)]
        steps += [(q0,) + b + (j == 0, j == len(blocks) - 1) for j, b in enumerate(blocks)]
    units = [(step, i) for step in steps for i in range(n)]
    qts = {}

    def scores(step, i):
        q0, st, klen, c_lo, masked, first, _ = step
        if first and i == 0:
            qts[q0] = [_transpose_bf16(q_ref[q0:q0 + tq, j * LANES:(j + 1) * LANES]) for j in range(n)]
        s = _dot(k_ref[st:st + klen, k_los[i]:k_los[i] + LANES], qts[q0][i][:, c_lo:])
        if masked:
            r = lax.broadcasted_iota(jnp.int32, s.shape, 0)
            c = lax.broadcasted_iota(jnp.int32, s.shape, 1)
            s = jnp.where(r <= c, s, -jnp.inf)
        return s

    state = [None] * n
    pending = [scores(*u) for u in units[:lookahead]]
    for t, (step, i) in enumerate(units):
        q0, st, klen, c_lo, masked, first, last = step
        s = pending.pop(0)
        if t + lookahead < len(units):
            pending.append(scores(*units[t + lookahead]))
        vt = vts[i][:, st:st + klen]
        m_blk = jnp.max(s, axis=0, keepdims=True)
        if first:
            m_new = m_blk
            acc = _dot(vt, jnp.exp2(s - m_new).astype(BF16))
        else:
            m, acc = state[i]
            m_new = jnp.maximum(m[:, c_lo:], m_blk)
            p = jnp.exp2(s - m_new).astype(BF16)
            acc_new = jnp.exp2(m[:, c_lo:] - m_new) * acc[:, c_lo:] + _dot(vt, p)
            if c_lo:
                m_new = jnp.concatenate([m[:, :c_lo], m_new], axis=1)
                acc_new = jnp.concatenate([acc[:, :c_lo], acc_new], axis=1)
            acc = acc_new
        state[i] = (m_new, acc)
        if last and i == n - 1:
            finish(q0, [a for _, a in state])


def _mla_attn_kernel(q_ref, k_ref, v_ref, o_ref, vt_ref):
    dv = D_MLA_V
    seq = q_ref.shape[0]
    vt = v_ref[...].astype(F32).T
    ones = jnp.ones((ONES_ROWS, seq), BF16)
    for hh in range(2):
        vt_ref[hh, :dv, :] = vt[hh * dv:(hh + 1) * dv].astype(BF16)
        vt_ref[hh, dv:, :] = ones

    def finish(q0, accs):
        ot = jnp.concatenate([a[:dv] / a[dv:dv + 1] for a in accs], axis=0)
        o_ref[q0:q0 + MLA_TQ, :] = ot.T.astype(BF16)

    _causal_attention_t(q_ref, [0, LANES], k_ref, [vt_ref.at[0], vt_ref.at[1]], finish, MLA_TQ, MLA_TK, MLA_LOOKAHEAD)


def _diff_attn_kernel(lam_ref, sg_ref, q_ref, k_ref, v_ref, o_ref, vt_ref, *, lam_init):
    dv = 2 * D_DIFF_HEAD
    seq = q_ref.shape[0]
    vt_ref[:dv, :] = _transpose_bf16(v_ref[...])
    vt_ref[dv:, :] = jnp.ones((ONES_ROWS, seq), BF16)
    lp = lam_ref[...]
    lam = (jnp.exp(jnp.sum(lp[0:1] * lp[1:2], axis=-1, keepdims=True))
           - jnp.exp(jnp.sum(lp[2:3] * lp[3:4], axis=-1, keepdims=True)) + lam_init)
    gain = sg_ref[...] * (1.0 - lam_init)

    def finish(q0, accs):
        a1, a2 = accs
        o = a1[:dv] / a1[dv:dv + 1] - lam * (a2[:dv] / a2[dv:dv + 1])
        on = o * lax.rsqrt(jnp.mean(o * o, axis=0, keepdims=True) + EPS)
        o_ref[q0:q0 + DIFF_TQ, :] = (on.T * gain).astype(BF16)

    _causal_attention_t(q_ref, [0, 0], k_ref, [vt_ref, vt_ref], finish, DIFF_TQ, DIFF_TK, DIFF_LOOKAHEAD)


def _attn_params():
    return pltpu.CompilerParams(
        dimension_semantics=("arbitrary", "arbitrary"), vmem_limit_bytes=VMEM_LIMIT)


def _mla_attn_call(qm, km, vm, batch, seq):
    t = qm.shape[0]
    pairs = N_MLA_HEADS // 2
    return pl.pallas_call(
        _mla_attn_kernel,
        grid=(batch, pairs),
        in_specs=[pl.BlockSpec((seq, 2 * LANES), lambda b, p: (b, p)),
                  pl.BlockSpec((seq, 2 * LANES), lambda b, p: (b, p)),
                  pl.BlockSpec((seq, LANES), lambda b, p: (b, p))],
        out_specs=pl.BlockSpec((seq, LANES), lambda b, p: (b, p)),
        out_shape=jax.ShapeDtypeStruct((t, N_MLA_HEADS * D_MLA_V), BF16),
        scratch_shapes=[pltpu.VMEM((2, D_MLA_V + ONES_ROWS, seq), BF16)],
        compiler_params=_attn_params(),
        name="mla_attn",
    )(qm, km, vm)


def _diff_attn_call(lam_p, sg, qd, kd, vd, batch, seq, lam_init):
    t = qd.shape[0]
    return pl.pallas_call(
        functools.partial(_diff_attn_kernel, lam_init=lam_init),
        grid=(batch, N_DIFF_HEADS),
        in_specs=[pl.BlockSpec(lam_p.shape, lambda b, j: (0, 0)),
                  pl.BlockSpec(sg.shape, lambda b, j: (0, 0)),
                  pl.BlockSpec((seq, 2 * LANES), lambda b, j: (b, j)),
                  pl.BlockSpec((seq, LANES), lambda b, j: (b, j)),
                  pl.BlockSpec((seq, LANES), lambda b, j: (b, j))],
        out_specs=pl.BlockSpec((seq, LANES), lambda b, j: (b, j)),
        out_shape=jax.ShapeDtypeStruct((t, C_DV), BF16),
        scratch_shapes=[pltpu.VMEM((2 * D_DIFF_HEAD + ONES_ROWS, seq), BF16)],
        compiler_params=_attn_params(),
        name="diff_attn",
    )(lam_p, sg, qd, kd, vd)


def _post_kernel(x_ref, om_ref, od_ref, gate_ref, wbm_ref, wbd_ref, wo_ref, gpa_ref, gpm_ref,
                 wup_ref, wdn_ref, gpo_ref, o_ref):
    subs = [slice(r0, r0 + SUB_POST) for r0 in range(0, x_ref.shape[0], SUB_POST)]
    u = [(_dot(om_ref[s, :], wbm_ref[...]), _dot(od_ref[s, :], wbd_ref[...])) for s in subs]
    y = []
    for s, (u_mla, u_diff) in zip(subs, u):
        mixed = (gate_ref[s, :D_MODEL].astype(F32) * u_mla
                 + gate_ref[s, D_MODEL:].astype(F32) * u_diff).astype(BF16)
        y.append(_dot(mixed, wo_ref[...]))
    x1 = [x_ref[s, :] + _rms(yy, gpa_ref[...]) for s, yy in zip(subs, y)]
    h = [_rms(xx, gpm_ref[...]).astype(BF16) for xx in x1]
    m = [None] * len(subs)
    for c0 in range(0, D_FF, FF_CHUNK):
        ups = [jnp.maximum(_dot(hh, wup_ref[:, c0:c0 + FF_CHUNK]), 0.0) for hh in h]
        for i, up in enumerate(ups):
            part = _dot((up * up).astype(BF16), wdn_ref[c0:c0 + FF_CHUNK, :])
            m[i] = part if m[i] is None else m[i] + part
    for s, xx, mm in zip(subs, x1, m):
        o_ref[s, :] = xx + _rms(mm, gpo_ref[...])


def _post_call(x2, om, od, gate, wbm, wbd, wo, gpa, gpm, wup, wdn, gpo):
    t = x2.shape[0]
    tm = TM_POST
    row = lambda c: pl.BlockSpec((tm, c), lambda i: (i, 0))
    return pl.pallas_call(
        _post_kernel,
        grid=(t // tm,),
        in_specs=[row(D_MODEL), row(om.shape[1]), row(od.shape[1]), row(gate.shape[1]),
                  _const_spec(wbm.shape), _const_spec(wbd.shape), _const_spec(wo.shape),
                  _const_spec((1, D_MODEL)), _const_spec((1, D_MODEL)),
                  _const_spec(wup.shape), _const_spec(wdn.shape), _const_spec((1, D_MODEL))],
        out_specs=row(D_MODEL),
        out_shape=jax.ShapeDtypeStruct((t, D_MODEL), F32),
        compiler_params=pltpu.CompilerParams(
            dimension_semantics=("arbitrary",), vmem_limit_bytes=VMEM_LIMIT),
        name="post",
    )(x2, om, od, gate, wbm, wbd, wo, gpa, gpm, wup, wdn, gpo)


def _rot_half_cols(w):
    d = w.shape[-1]
    return jnp.concatenate([-w[..., d // 2:], w[..., :d // 2]], axis=-1)


def _rope_tables(seq):
    pos = np.arange(seq, dtype=np.float64)

    def cs(d):
        inv = 1.0 / (ROPE_THETA ** (np.arange(0, d, 2, dtype=np.float64) / d))
        ang = pos[:, None] * inv[None, :]
        return np.cos(ang), np.sin(ang)

    cd, sd = cs(D_DIFF_HEAD)
    cosd = np.tile(cd, (1, 4))
    sind = np.concatenate([-sd, sd, -sd, sd], axis=1)
    cm, sm = cs(D_MLA_ROPE)
    one = np.ones((seq, D_MLA_NOPE))
    zero_n = np.zeros((seq, D_MLA_NOPE))
    zero_p = np.zeros((seq, LANES - D_MLA_NOPE - D_MLA_ROPE))
    cosm = np.concatenate([one, cm, cm, zero_p], axis=1)
    sinm = np.concatenate([zero_n, sm, sm, zero_p], axis=1)
    return tuple(jnp.asarray(t, dtype=F32) for t in (cosd, sind, cosm, sinm))


def _layout_in_proj(w_in):
    w = w_in.astype(BF16)
    d = w.shape[0]
    s0 = C_QA
    s1 = s0 + C_KVA
    w_kr = w[:, s0 + KV_LORA:s1]
    zn = jnp.zeros((d, D_MLA_NOPE), BF16)
    zp = jnp.zeros((d, LANES - D_MLA_NOPE - D_MLA_ROPE), BF16)
    w_all = jnp.concatenate(
        [w[:, :s0 + KV_LORA], zn, w_kr, zp, zn, _rot_half_cols(w_kr), zp, w[:, s1:]], axis=1)
    assert w_all.shape[1] == W_ALL_COLS
    return w_all


def _layout_q_b(w_q_b):
    r = w_q_b.shape[0]
    w = w_q_b.reshape(r, N_MLA_HEADS, D_MLA_NOPE + D_MLA_ROPE)
    nope, rp = w[..., :D_MLA_NOPE], w[..., D_MLA_NOPE:]
    zp = jnp.zeros((r, N_MLA_HEADS, LANES - D_MLA_NOPE - D_MLA_ROPE), w.dtype)
    zn = jnp.zeros((r, N_MLA_HEADS, D_MLA_NOPE), w.dtype)
    a = jnp.concatenate([nope, rp, zp], axis=-1).reshape(r, N_MLA_HEADS * LANES)
    b = jnp.concatenate([zn, _rot_half_cols(rp), zp], axis=-1).reshape(r, N_MLA_HEADS * LANES)
    return jnp.concatenate([a, b], axis=1).astype(BF16)


def _layout_kv_b(w_kv_b):
    r = w_kv_b.shape[0]
    w = w_kv_b.reshape(r, N_MLA_HEADS, D_MLA_NOPE + D_MLA_V)
    kn, v = w[..., :D_MLA_NOPE], w[..., D_MLA_NOPE:]
    zk = jnp.zeros((r, N_MLA_HEADS, LANES - D_MLA_NOPE), w.dtype)
    wk = jnp.concatenate([kn, zk], axis=-1).reshape(r, N_MLA_HEADS * LANES)
    wv = v.reshape(r, N_MLA_HEADS * D_MLA_V)
    return jnp.concatenate([wk, wv], axis=1).astype(BF16)


def kernel(x, pre_attn_g, w_in, q_norm_g, w_q_b, kv_norm_g, w_kv_b, lambda_q1, lambda_k1, lambda_q2, lambda_k2, subln_g, w_br_mla, w_br_diff, w_out, post_attn_g, pre_mlp_g, w_mlp_up, w_mlp_down, post_mlp_g):
    batch, seq, d_model = x.shape
    depth = w_in.shape[0]
    assert d_model == D_MODEL and seq % MLA_TQ == 0 and seq % DIFF_TQ == 0 and seq % TM_PROJ == 0
    assert MLA_TQ % MLA_TK == 0 and DIFF_TQ % DIFF_TK == 0 and MLA_TQ % DIAG_TK == 0 and DIFF_TQ % DIAG_TK == 0
    assert (batch * seq) % TM_POST == 0
    cosd, sind, cosm, sinm = _rope_tables(seq)
    x2 = x.reshape(batch * seq, d_model)
    row = lambda v: v.reshape(1, -1).astype(F32)
    for l in range(depth):
        lam_init = 0.8 - 0.6 * math.exp(-0.3 * l)
        qm, km, vm, qd, kd, vd, gate = _proj_call(
            x2, row(pre_attn_g[l]), _layout_in_proj(w_in[l]), row(q_norm_g[l]),
            _layout_q_b(w_q_b[l]), row(kv_norm_g[l]), _layout_kv_b(w_kv_b[l]),
            cosd, sind, cosm, sinm, seq)
        om = _mla_attn_call(qm, km, vm, batch, seq)
        lam_p = jnp.stack([lambda_q1[l], lambda_k1[l], lambda_q2[l], lambda_k2[l]]).astype(F32)
        od = _diff_attn_call(lam_p, row(subln_g[l]), qd, kd, vd, batch, seq, lam_init)
        x2 = _post_call(
            x2, om, od, gate, w_br_mla[l].astype(BF16), w_br_diff[l].astype(BF16),
            w_out[l].astype(BF16), row(post_attn_g[l]), row(pre_mlp_g[l]),
            w_mlp_up[l].astype(BF16), w_mlp_down[l].astype(BF16), row(post_mlp_g[l]))
    return x2.reshape(batch, seq, d_model)
```

```python
import functools
import math

import numpy as np
import jax
import jax.numpy as jnp
from jax import lax
from jax.experimental import pallas as pl
from jax.experimental.pallas import tpu as pltpu

D_MODEL = 1024
N_MLA_HEADS = 8
D_MLA_NOPE = 64
D_MLA_ROPE = 32
D_MLA_V = 64
Q_LORA = 256
KV_LORA = 128
N_DIFF_HEADS = 8
D_DIFF_HEAD = 64
D_FF = 4 * D_MODEL
ROPE_THETA = 10000.0
EPS = 1e-6

LANES = 128
VMEM_LIMIT = 52 * 1024 * 1024

C_QA = Q_LORA
C_KVA = KV_LORA + D_MLA_ROPE
C_DQ = 2 * N_DIFF_HEADS * D_DIFF_HEAD
C_DK = C_DQ
C_DV = N_DIFF_HEADS * 2 * D_DIFF_HEAD
C_GATE = 2 * D_MODEL

O_QA = 0
O_CKV = O_QA + Q_LORA
O_KR = O_CKV + KV_LORA
O_DQ = O_KR + 2 * LANES
O_DK = O_DQ + C_DQ
O_DV = O_DK + C_DK
O_GATE = O_DV + C_DV
W_ALL_COLS = O_GATE + C_GATE

TM_PROJ = 512
SUB_PROJ = 256
TM_POST = 512
SUB_POST = 256
FF_CHUNK = 1024
MLA_TQ, MLA_TK, MLA_LOOKAHEAD = 512, 256, 3
DIFF_TQ, DIFF_TK, DIFF_LOOKAHEAD = 1024, 512, 2
DIAG_TK = 256

LOG2E = math.log2(math.e)

F32 = jnp.float32
BF16 = jnp.bfloat16


def _rms(x, g):
    return x * lax.rsqrt(jnp.mean(x * x, axis=-1, keepdims=True) + EPS) * g


def _dot(a, b):
    return jnp.dot(a, b, preferred_element_type=F32)


def _dot_nt(a, b):
    return lax.dot_general(a, b, (((1,), (1,)), ((), ())), preferred_element_type=F32)


def _proj_kernel(x_ref, g_ref, w_ref, qng_ref, wqb_ref, kvg_ref, wkvb_ref,
                 cosd_ref, sind_ref, cosm_ref, sinm_ref,
                 qm_ref, km_ref, vm_ref, qd_ref, kd_ref, vd_ref, gate_ref,
                 *, mla_scale, diff_scale):
    subs = [slice(r0, r0 + SUB_PROJ) for r0 in range(0, x_ref.shape[0], SUB_PROJ)]
    half = N_MLA_HEADS * LANES
    lane = lax.broadcasted_iota(jnp.int32, (1, LANES), 1)
    low_half = lane % D_DIFF_HEAD < D_DIFF_HEAD // 2
    mask_a = jnp.where(lane < D_DIFF_HEAD, diff_scale, 0.0).astype(F32)
    mask_b = jnp.where(lane < D_DIFF_HEAD, 0.0, diff_scale).astype(F32)

    def seg(hh, a, b):
        return _dot(hh, w_ref[:, a:b])

    def rope_group(xg, cosd, sind):
        up = pltpu.roll(xg, LANES - D_DIFF_HEAD // 2, 1)
        down = pltpu.roll(xg, D_DIFF_HEAD // 2, 1)
        return xg * cosd + jnp.where(low_half, up, down) * sind

    h = [_rms(x_ref[s, :], g_ref[...]).astype(BF16) for s in subs]
    lat = [(seg(hh, O_QA, O_QA + Q_LORA), seg(hh, O_CKV, O_CKV + KV_LORA),
            seg(hh, O_KR, O_KR + 2 * LANES), seg(hh, O_GATE, O_GATE + C_GATE)) for hh in h]

    up_proj = []
    for s, hh, (qa, ckv, kr, gl) in zip(subs, h, lat):
        qn = _rms(qa, qng_ref[...]).astype(BF16)
        kn = _rms(ckv, kvg_ref[...]).astype(BF16)
        krp = kr[:, :LANES] * cosm_ref[s, :] + kr[:, LANES:] * sinm_ref[s, :]
        up_proj.append((_dot(qn, wqb_ref[...]), _dot(kn, wkvb_ref[...]), krp, seg(hh, O_DQ, O_DQ + C_DQ)))
        gate_ref[s, :] = (1.0 / (1.0 + jnp.exp(-gl))).astype(BF16)

    dks = []
    for s, hh, (qab, kvb, krp, dq) in zip(subs, h, up_proj):
        dks.append(seg(hh, O_DK, O_DK + C_DK))
        cosm = cosm_ref[s, :]
        sinm = sinm_ref[s, :]
        for j in range(N_MLA_HEADS):
            a = qab[:, j * LANES:(j + 1) * LANES]
            b = qab[:, half + j * LANES:half + (j + 1) * LANES]
            qm_ref[s, j * LANES:(j + 1) * LANES] = ((a * cosm + b * sinm) * mla_scale).astype(BF16)
        for j in range(N_MLA_HEADS):
            km_ref[s, j * LANES:(j + 1) * LANES] = (kvb[:, j * LANES:(j + 1) * LANES] + krp).astype(BF16)
        vm_ref[s, :] = kvb[:, half:].astype(BF16)
        for j in range(N_DIFF_HEADS):
            r = rope_group(dq[:, j * LANES:(j + 1) * LANES], cosd_ref[s, :], sind_ref[s, :])
            qd_ref[s, 2 * j * LANES:(2 * j + 1) * LANES] = (r * mask_a).astype(BF16)
            qd_ref[s, (2 * j + 1) * LANES:(2 * j + 2) * LANES] = (r * mask_b).astype(BF16)

    for s, hh, dk in zip(subs, h, dks):
        dv = seg(hh, O_DV, O_DV + C_DV)
        for j in range(N_DIFF_HEADS):
            kd_ref[s, j * LANES:(j + 1) * LANES] = rope_group(
                dk[:, j * LANES:(j + 1) * LANES], cosd_ref[s, :], sind_ref[s, :]).astype(BF16)
        vd_ref[s, :] = dv.astype(BF16)


def _const_spec(shape):
    return pl.BlockSpec(shape, lambda i: (0,) * len(shape), pipeline_mode=pl.Buffered(1))


def _proj_call(x2, g, w_all, qng, wqb, kvg, wkvb, cosd, sind, cosm, sinm, seq):
    t = x2.shape[0]
    tm = TM_PROJ
    tiles_per_seq = seq // tm
    row = lambda c: pl.BlockSpec((tm, c), lambda i: (i, 0))
    tab = pl.BlockSpec((tm, LANES), lambda i: (i % tiles_per_seq, 0))
    widths = (N_MLA_HEADS * LANES, N_MLA_HEADS * LANES, N_MLA_HEADS * D_MLA_V,
              2 * C_DQ, C_DK, C_DV, C_GATE)
    kern = functools.partial(
        _proj_kernel,
        mla_scale=LOG2E / math.sqrt(D_MLA_NOPE + D_MLA_ROPE),
        diff_scale=LOG2E / math.sqrt(D_DIFF_HEAD))
    return pl.pallas_call(
        kern,
        grid=(t // tm,),
        in_specs=[row(D_MODEL), _const_spec((1, D_MODEL)), _const_spec(w_all.shape),
                  _const_spec((1, Q_LORA)), _const_spec(wqb.shape),
                  _const_spec((1, KV_LORA)), _const_spec(wkvb.shape),
                  tab, tab, tab, tab],
        out_specs=[row(c) for c in widths],
        out_shape=[jax.ShapeDtypeStruct((t, c), BF16) for c in widths],
        compiler_params=pltpu.CompilerParams(
            dimension_semantics=("arbitrary",), vmem_limit_bytes=VMEM_LIMIT),
        name="proj",
    )(x2, g, w_all, qng, wqb, kvg, wkvb, cosd, sind, cosm, sinm)


ONES_ROWS = 16


def _transpose_bf16(x):
    return x.astype(F32).T.astype(BF16)


def _causal_attention_t(q_ref, k_los, k_ref, vts, finish, tq, tk, lookahead):
    n = len(k_los)
    seq = q_ref.shape[0]
    steps = []
    for qi in range(seq // tq):
        q0 = qi * tq
        blocks = [(kb * tk, tk, 0, False) for kb in range(q0 // tk)]
        blocks += [(q0 + d * DIAG_TK, DIAG_TK, d * DIAG_TK, True) for d in range(tq // DIAG_TK)]
        steps += [(q0,) + b + (j == 0, j == len(blocks) - 1) for j, b in enumerate(blocks)]
    units = [(step, i) for step in steps for i in range(n)]
    qts = {}

    def scores(step, i):
        q0, st, klen, c_lo, masked, first, _ = step
        if first and i == 0:
            qts[q0] = [_transpose_bf16(q_ref[q0:q0 + tq, j * LANES:(j + 1) * LANES]) for j in range(n)]
        s = _dot(k_ref[st:st + klen, k_los[i]:k_los[i] + LANES], qts[q0][i][:, c_lo:])
        if masked:
            r = lax.broadcasted_iota(jnp.int32, s.shape, 0)
            c = lax.broadcasted_iota(jnp.int32, s.shape, 1)
            s = jnp.where(r <= c, s, -jnp.inf)
        return s

    state = [None] * n
    pending = [scores(*u) for u in units[:lookahead]]
    for t, (step, i) in enumerate(units):
        q0, st, klen, c_lo, masked, first, last = step
        s = pending.pop(0)
        if t + lookahead < len(units):
            pending.append(scores(*units[t + lookahead]))
        vt = vts[i][:, st:st + klen]
        m_blk = jnp.max(s, axis=0, keepdims=True)
        if first:
            m_new = m_blk
            acc = _dot(vt, jnp.exp2(s - m_new).astype(BF16))
        else:
            m, acc = state[i]
            m_new = jnp.maximum(m[:, c_lo:], m_blk)
            p = jnp.exp2(s - m_new).astype(BF16)
            acc_new = jnp.exp2(m[:, c_lo:] - m_new) * acc[:, c_lo:] + _dot(vt, p)
            if c_lo:
                m_new = jnp.concatenate([m[:, :c_lo], m_new], axis=1)
                acc_new = jnp.concatenate([acc[:, :c_lo], acc_new], axis=1)
            acc = acc_new
        state[i] = (m_new, acc)
        if last and i == n - 1:
            finish(q0, [a for _, a in state])


def _mla_attn_kernel(q_ref, k_ref, v_ref, o_ref, vt_ref):
    dv = D_MLA_V
    seq = q_ref.shape[0]
    vt = v_ref[...].astype(F32).T
    ones = jnp.ones((ONES_ROWS, seq), BF16)
    for hh in range(2):
        vt_ref[hh, :dv, :] = vt[hh * dv:(hh + 1) * dv].astype(BF16)
        vt_ref[hh, dv:, :] = ones

    def finish(q0, accs):
        ot = jnp.concatenate([a[:dv] / a[dv:dv + 1] for a in accs], axis=0)
        o_ref[q0:q0 + MLA_TQ, :] = ot.T.astype(BF16)

    _causal_attention_t(q_ref, [0, LANES], k_ref, [vt_ref.at[0], vt_ref.at[1]], finish, MLA_TQ, MLA_TK, MLA_LOOKAHEAD)


def _diff_attn_kernel(lam_ref, sg_ref, q_ref, k_ref, v_ref, o_ref, vt_ref, *, lam_init):
    dv = 2 * D_DIFF_HEAD
    seq = q_ref.shape[0]
    vt_ref[:dv, :] = _transpose_bf16(v_ref[...])
    vt_ref[dv:, :] = jnp.ones((ONES_ROWS, seq), BF16)
    lp = lam_ref[...]
    lam = (jnp.exp(jnp.sum(lp[0:1] * lp[1:2], axis=-1, keepdims=True))
           - jnp.exp(jnp.sum(lp[2:3] * lp[3:4], axis=-1, keepdims=True)) + lam_init)
    gain = sg_ref[...] * (1.0 - lam_init)

    def finish(q0, accs):
        a1, a2 = accs
        o = a1[:dv] / a1[dv:dv + 1] - lam * (a2[:dv] / a2[dv:dv + 1])
        on = o * lax.rsqrt(jnp.mean(o * o, axis=0, keepdims=True) + EPS)
        o_ref[q0:q0 + DIFF_TQ, :] = (on.T * gain).astype(BF16)

    _causal_attention_t(q_ref, [0, 0], k_ref, [vt_ref, vt_ref], finish, DIFF_TQ, DIFF_TK, DIFF_LOOKAHEAD)


def _attn_params():
    return pltpu.CompilerParams(
        dimension_semantics=("arbitrary", "arbitrary"), vmem_limit_bytes=VMEM_LIMIT)


def _mla_attn_call(qm, km, vm, batch, seq):
    t = qm.shape[0]
    pairs = N_MLA_HEADS // 2
    return pl.pallas_call(
        _mla_attn_kernel,
        grid=(batch, pairs),
        in_specs=[pl.BlockSpec((seq, 2 * LANES), lambda b, p: (b, p)),
                  pl.BlockSpec((seq, 2 * LANES), lambda b, p: (b, p)),
                  pl.BlockSpec((seq, LANES), lambda b, p: (b, p))],
        out_specs=pl.BlockSpec((seq, LANES), lambda b, p: (b, p)),
        out_shape=jax.ShapeDtypeStruct((t, N_MLA_HEADS * D_MLA_V), BF16),
        scratch_shapes=[pltpu.VMEM((2, D_MLA_V + ONES_ROWS, seq), BF16)],
        compiler_params=_attn_params(),
        name="mla_attn",
    )(qm, km, vm)


def _diff_attn_call(lam_p, sg, qd, kd, vd, batch, seq, lam_init):
    t = qd.shape[0]
    return pl.pallas_call(
        functools.partial(_diff_attn_kernel, lam_init=lam_init),
        grid=(batch, N_DIFF_HEADS),
        in_specs=[pl.BlockSpec(lam_p.shape, lambda b, j: (0, 0)),
                  pl.BlockSpec(sg.shape, lambda b, j: (0, 0)),
                  pl.BlockSpec((seq, 2 * LANES), lambda b, j: (b, j)),
                  pl.BlockSpec((seq, LANES), lambda b, j: (b, j)),
                  pl.BlockSpec((seq, LANES), lambda b, j: (b, j))],
        out_specs=pl.BlockSpec((seq, LANES), lambda b, j: (b, j)),
        out_shape=jax.ShapeDtypeStruct((t, C_DV), BF16),
        scratch_shapes=[pltpu.VMEM((2 * D_DIFF_HEAD + ONES_ROWS, seq), BF16)],
        compiler_params=_attn_params(),
        name="diff_attn",
    )(lam_p, sg, qd, kd, vd)


def _post_kernel(x_ref, om_ref, od_ref, gate_ref, wbm_ref, wbd_ref, wo_ref, gpa_ref, gpm_ref,
                 wup_ref, wdn_ref, gpo_ref, o_ref):
    subs = [slice(r0, r0 + SUB_POST) for r0 in range(0, x_ref.shape[0], SUB_POST)]
    u = [(_dot(om_ref[s, :], wbm_ref[...]), _dot(od_ref[s, :], wbd_ref[...])) for s in subs]
    y = []
    for s, (u_mla, u_diff) in zip(subs, u):
        mixed = (gate_ref[s, :D_MODEL].astype(F32) * u_mla
                 + gate_ref[s, D_MODEL:].astype(F32) * u_diff).astype(BF16)
        y.append(_dot(mixed, wo_ref[...]))
    x1 = [x_ref[s, :] + _rms(yy, gpa_ref[...]) for s, yy in zip(subs, y)]
    h = [_rms(xx, gpm_ref[...]).astype(BF16) for xx in x1]
    m = [None] * len(subs)
    for c0 in range(0, D_FF, FF_CHUNK):
        ups = [jnp.maximum(_dot(hh, wup_ref[:, c0:c0 + FF_CHUNK]), 0.0) for hh in h]
        for i, up in enumerate(ups):
            part = _dot((up * up).astype(BF16), wdn_ref[c0:c0 + FF_CHUNK, :])
            m[i] = part if m[i] is None else m[i] + part
    for s, xx, mm in zip(subs, x1, m):
        o_ref[s, :] = xx + _rms(mm, gpo_ref[...])


def _post_call(x2, om, od, gate, wbm, wbd, wo, gpa, gpm, wup, wdn, gpo):
    t = x2.shape[0]
    tm = TM_POST
    row = lambda c: pl.BlockSpec((tm, c), lambda i: (i, 0))
    return pl.pallas_call(
        _post_kernel,
        grid=(t // tm,),
        in_specs=[row(D_MODEL), row(om.shape[1]), row(od.shape[1]), row(gate.shape[1]),
                  _const_spec(wbm.shape), _const_spec(wbd.shape), _const_spec(wo.shape),
                  _const_spec((1, D_MODEL)), _const_spec((1, D_MODEL)),
                  _const_spec(wup.shape), _const_spec(wdn.shape), _const_spec((1, D_MODEL))],
        out_specs=row(D_MODEL),
        out_shape=jax.ShapeDtypeStruct((t, D_MODEL), F32),
        compiler_params=pltpu.CompilerParams(
            dimension_semantics=("arbitrary",), vmem_limit_bytes=VMEM_LIMIT),
        name="post",
    )(x2, om, od, gate, wbm, wbd, wo, gpa, gpm, wup, wdn, gpo)


def _rot_half_cols(w):
    d = w.shape[-1]
    return jnp.concatenate([-w[..., d // 2:], w[..., :d // 2]], axis=-1)


def _rope_tables(seq):
    pos = np.arange(seq, dtype=np.float64)

    def cs(d):
        inv = 1.0 / (ROPE_THETA ** (np.arange(0, d, 2, dtype=np.float64) / d))
        ang = pos[:, None] * inv[None, :]
        return np.cos(ang), np.sin(ang)

    cd, sd = cs(D_DIFF_HEAD)
    cosd = np.tile(cd, (1, 4))
    sind = np.concatenate([-sd, sd, -sd, sd], axis=1)
    cm, sm = cs(D_MLA_ROPE)
    one = np.ones((seq, D_MLA_NOPE))
    zero_n = np.zeros((seq, D_MLA_NOPE))
    zero_p = np.zeros((seq, LANES - D_MLA_NOPE - D_MLA_ROPE))
    cosm = np.concatenate([one, cm, cm, zero_p], axis=1)
    sinm = np.concatenate([zero_n, sm, sm, zero_p], axis=1)
    return tuple(jnp.asarray(t, dtype=F32) for t in (cosd, sind, cosm, sinm))


def _w_in_layout_kernel(w_ref, o_ref):
    s0 = C_QA + KV_LORA
    o_ref[:, :s0] = w_ref[:, :s0].astype(BF16)
    grp = w_ref[:, s0:s0 + LANES]
    lane = lax.broadcasted_iota(jnp.int32, (1, LANES), 1)
    half = D_MLA_ROPE // 2
    r64 = pltpu.roll(grp, D_MLA_NOPE, 1)
    up = pltpu.roll(r64, LANES - half, 1)
    down = pltpu.roll(r64, half, 1)
    lo = (lane >= D_MLA_NOPE) & (lane < D_MLA_NOPE + half)
    hi = (lane >= D_MLA_NOPE + half) & (lane < D_MLA_NOPE + D_MLA_ROPE)
    o_ref[:, O_KR:O_KR + LANES] = jnp.where(lo | hi, r64, 0.0).astype(BF16)
    o_ref[:, O_KR + LANES:O_DQ] = jnp.where(lo, -up, jnp.where(hi, down, 0.0)).astype(BF16)
    o_ref[:, O_DQ:] = w_ref[:, C_QA + C_KVA:].astype(BF16)


def _layout_in_proj(w_in, l):
    d = w_in.shape[1]
    rows = 128
    return pl.pallas_call(
        _w_in_layout_kernel,
        grid=(d // rows,),
        in_specs=[pl.BlockSpec((None, rows, w_in.shape[2]), lambda i: (l, i, 0))],
        out_specs=pl.BlockSpec((rows, W_ALL_COLS), lambda i: (i, 0)),
        out_shape=jax.ShapeDtypeStruct((d, W_ALL_COLS), BF16),
        compiler_params=pltpu.CompilerParams(
            dimension_semantics=("arbitrary",), vmem_limit_bytes=VMEM_LIMIT),
        name="w_in_layout",
    )(w_in)


def _layout_q_b(w_q_b):
    r = w_q_b.shape[0]
    w = w_q_b.reshape(r, N_MLA_HEADS, D_MLA_NOPE + D_MLA_ROPE)
    nope, rp = w[..., :D_MLA_NOPE], w[..., D_MLA_NOPE:]
    zp = jnp.zeros((r, N_MLA_HEADS, LANES - D_MLA_NOPE - D_MLA_ROPE), w.dtype)
    zn = jnp.zeros((r, N_MLA_HEADS, D_MLA_NOPE), w.dtype)
    a = jnp.concatenate([nope, rp, zp], axis=-1).reshape(r, N_MLA_HEADS * LANES)
    b = jnp.concatenate([zn, _rot_half_cols(rp), zp], axis=-1).reshape(r, N_MLA_HEADS * LANES)
    return jnp.concatenate([a, b], axis=1).astype(BF16)


def _layout_kv_b(w_kv_b):
    r = w_kv_b.shape[0]
    w = w_kv_b.reshape(r, N_MLA_HEADS, D_MLA_NOPE + D_MLA_V)
    kn, v = w[..., :D_MLA_NOPE], w[..., D_MLA_NOPE:]
    zk = jnp.zeros((r, N_MLA_HEADS, LANES - D_MLA_NOPE), w.dtype)
    wk = jnp.concatenate([kn, zk], axis=-1).reshape(r, N_MLA_HEADS * LANES)
    wv = v.reshape(r, N_MLA_HEADS * D_MLA_V)
    return jnp.concatenate([wk, wv], axis=1).astype(BF16)


def kernel(x, pre_attn_g, w_in, q_norm_g, w_q_b, kv_norm_g, w_kv_b, lambda_q1, lambda_k1, lambda_q2, lambda_k2, subln_g, w_br_mla, w_br_diff, w_out, post_attn_g, pre_mlp_g, w_mlp_up, w_mlp_down, post_mlp_g):
    batch, seq, d_model = x.shape
    depth = w_in.shape[0]
    assert d_model == D_MODEL and seq % MLA_TQ == 0 and seq % DIFF_TQ == 0 and seq % TM_PROJ == 0
    assert MLA_TQ % MLA_TK == 0 and DIFF_TQ % DIFF_TK == 0 and MLA_TQ % DIAG_TK == 0 and DIFF_TQ % DIAG_TK == 0
    assert (batch * seq) % TM_POST == 0
    cosd, sind, cosm, sinm = _rope_tables(seq)
    x2 = x.reshape(batch * seq, d_model)
    row = lambda v: v.reshape(1, -1).astype(F32)
    for l in range(depth):
        lam_init = 0.8 - 0.6 * math.exp(-0.3 * l)
        qm, km, vm, qd, kd, vd, gate = _proj_call(
            x2, row(pre_attn_g[l]), _layout_in_proj(w_in, l), row(q_norm_g[l]),
            _layout_q_b(w_q_b[l]), row(kv_norm_g[l]), _layout_kv_b(w_kv_b[l]),
            cosd, sind, cosm, sinm, seq)
        om = _mla_attn_call(qm, km, vm, batch, seq)
        lam_p = jnp.stack([lambda_q1[l], lambda_k1[l], lambda_q2[l], lambda_k2[l]]).astype(F32)
        od = _diff_attn_call(lam_p, row(subln_g[l]), qd, kd, vd, batch, seq, lam_init)
        x2 = _post_call(
            x2, om, od, gate, w_br_mla[l].astype(BF16), w_br_diff[l].astype(BF16),
            w_out[l].astype(BF16), row(post_attn_g[l]), row(pre_mlp_g[l]),
            w_mlp_up[l].astype(BF16), w_mlp_down[l].astype(BF16), row(post_mlp_g[l]))
    return x2.reshape(batch, seq, d_model)
```

```python
import functools
import math

import numpy as np
import jax
import jax.numpy as jnp
from jax import lax
from jax.experimental import pallas as pl
from jax.experimental.pallas import tpu as pltpu

D_MODEL = 1024
N_MLA_HEADS = 8
D_MLA_NOPE = 64
D_MLA_ROPE = 32
D_MLA_V = 64
Q_LORA = 256
KV_LORA = 128
N_DIFF_HEADS = 8
D_DIFF_HEAD = 64
D_FF = 4 * D_MODEL
ROPE_THETA = 10000.0
EPS = 1e-6

LANES = 128
VMEM_LIMIT = 52 * 1024 * 1024

C_QA = Q_LORA
C_KVA = KV_LORA + D_MLA_ROPE
C_DQ = 2 * N_DIFF_HEADS * D_DIFF_HEAD
C_DK = C_DQ
C_DV = N_DIFF_HEADS * 2 * D_DIFF_HEAD
C_GATE = 2 * D_MODEL

R_QA = 0
R_CKV = R_QA + Q_LORA
R_KR = R_CKV + KV_LORA
R_DQ = R_KR + D_MLA_ROPE
R_DK = R_DQ + C_DQ
R_DV = R_DK + C_DK
R_GATE = R_DV + C_DV
D_IN_TOTAL = R_GATE + C_GATE

TM_PROJ = 512
SUB_PROJ = 256
TM_POST = 512
SUB_POST = 256
FF_CHUNK = 1024
MLA_TQ, MLA_TK, MLA_LOOKAHEAD = 512, 256, 3
DIFF_TQ, DIFF_TK, DIFF_LOOKAHEAD = 1024, 512, 2
DIAG_TK = 256

LOG2E = math.log2(math.e)

F32 = jnp.float32
BF16 = jnp.bfloat16


def _rms(x, g):
    return x * lax.rsqrt(jnp.mean(x * x, axis=-1, keepdims=True) + EPS) * g


def _dot(a, b):
    return jnp.dot(a, b, preferred_element_type=F32)


def _dot_nt(a, b):
    return lax.dot_general(a, b, (((1,), (1,)), ((), ())), preferred_element_type=F32)


def _proj_kernel(x_ref, g_ref, wt_ref, krt_ref, qng_ref, wqb_ref, kvg_ref, wkvb_ref,
                 cosd_ref, sind_ref, cosm_ref, sinm_ref,
                 qm_ref, km_ref, vm_ref, qd_ref, kd_ref, vd_ref, gate_ref,
                 *, mla_scale, diff_scale):
    subs = [slice(r0, r0 + SUB_PROJ) for r0 in range(0, x_ref.shape[0], SUB_PROJ)]
    half = N_MLA_HEADS * LANES
    lane = lax.broadcasted_iota(jnp.int32, (1, LANES), 1)
    low_half = lane % D_DIFF_HEAD < D_DIFF_HEAD // 2
    mask_a = jnp.where(lane < D_DIFF_HEAD, diff_scale, 0.0).astype(F32)
    mask_b = jnp.where(lane < D_DIFF_HEAD, 0.0, diff_scale).astype(F32)

    def seg(hh, r0, n):
        return _dot_nt(hh, wt_ref[r0:r0 + n, :])

    def rope_group(xg, cosd, sind):
        up = pltpu.roll(xg, LANES - D_DIFF_HEAD // 2, 1)
        down = pltpu.roll(xg, D_DIFF_HEAD // 2, 1)
        return xg * cosd + jnp.where(low_half, up, down) * sind

    h = [_rms(x_ref[s, :], g_ref[...]).astype(BF16) for s in subs]
    lat = [(seg(hh, R_QA, Q_LORA), seg(hh, R_CKV, KV_LORA),
            _dot_nt(hh, krt_ref[...]), seg(hh, R_GATE, C_GATE)) for hh in h]

    up_proj = []
    for s, hh, (qa, ckv, kr, gl) in zip(subs, h, lat):
        qn = _rms(qa, qng_ref[...]).astype(BF16)
        kn = _rms(ckv, kvg_ref[...]).astype(BF16)
        krp = kr[:, :LANES] * cosm_ref[s, :] + kr[:, LANES:] * sinm_ref[s, :]
        up_proj.append((_dot(qn, wqb_ref[...]), _dot(kn, wkvb_ref[...]), krp, seg(hh, R_DQ, C_DQ)))
        gate_ref[s, :] = (1.0 / (1.0 + jnp.exp(-gl))).astype(BF16)

    dks = []
    for s, hh, (qab, kvb, krp, dq) in zip(subs, h, up_proj):
        dks.append(seg(hh, R_DK, C_DK))
        cosm = cosm_ref[s, :]
        sinm = sinm_ref[s, :]
        for j in range(N_MLA_HEADS):
            a = qab[:, j * LANES:(j + 1) * LANES]
            b = qab[:, half + j * LANES:half + (j + 1) * LANES]
            qm_ref[s, j * LANES:(j + 1) * LANES] = ((a * cosm + b * sinm) * mla_scale).astype(BF16)
        for j in range(N_MLA_HEADS):
            km_ref[s, j * LANES:(j + 1) * LANES] = (kvb[:, j * LANES:(j + 1) * LANES] + krp).astype(BF16)
        vm_ref[s, :] = kvb[:, half:].astype(BF16)
        for j in range(N_DIFF_HEADS):
            r = rope_group(dq[:, j * LANES:(j + 1) * LANES], cosd_ref[s, :], sind_ref[s, :])
            qd_ref[s, 2 * j * LANES:(2 * j + 1) * LANES] = (r * mask_a).astype(BF16)
            qd_ref[s, (2 * j + 1) * LANES:(2 * j + 2) * LANES] = (r * mask_b).astype(BF16)

    for s, hh, dk in zip(subs, h, dks):
        dv = seg(hh, R_DV, C_DV)
        for j in range(N_DIFF_HEADS):
            kd_ref[s, j * LANES:(j + 1) * LANES] = rope_group(
                dk[:, j * LANES:(j + 1) * LANES], cosd_ref[s, :], sind_ref[s, :]).astype(BF16)
        vd_ref[s, :] = dv.astype(BF16)


def _const_spec(shape):
    return pl.BlockSpec(shape, lambda i: (0,) * len(shape), pipeline_mode=pl.Buffered(1))


def _proj_call(x2, g, wt, krt, qng, wqb, kvg, wkvb, cosd, sind, cosm, sinm, seq):
    t = x2.shape[0]
    tm = TM_PROJ
    tiles_per_seq = seq // tm
    row = lambda c: pl.BlockSpec((tm, c), lambda i: (i, 0))
    tab = pl.BlockSpec((tm, LANES), lambda i: (i % tiles_per_seq, 0))
    widths = (N_MLA_HEADS * LANES, N_MLA_HEADS * LANES, N_MLA_HEADS * D_MLA_V,
              2 * C_DQ, C_DK, C_DV, C_GATE)
    kern = functools.partial(
        _proj_kernel,
        mla_scale=LOG2E / math.sqrt(D_MLA_NOPE + D_MLA_ROPE),
        diff_scale=LOG2E / math.sqrt(D_DIFF_HEAD))
    return pl.pallas_call(
        kern,
        grid=(t // tm,),
        in_specs=[row(D_MODEL), _const_spec((1, D_MODEL)), _const_spec(wt.shape), _const_spec(krt.shape),
                  _const_spec((1, Q_LORA)), _const_spec(wqb.shape),
                  _const_spec((1, KV_LORA)), _const_spec(wkvb.shape),
                  tab, tab, tab, tab],
        out_specs=[row(c) for c in widths],
        out_shape=[jax.ShapeDtypeStruct((t, c), BF16) for c in widths],
        compiler_params=pltpu.CompilerParams(
            dimension_semantics=("arbitrary",), vmem_limit_bytes=VMEM_LIMIT),
        name="proj",
    )(x2, g, wt, krt, qng, wqb, kvg, wkvb, cosd, sind, cosm, sinm)


ONES_ROWS = 16


def _transpose_bf16(x):
    return x.astype(F32).T.astype(BF16)


def _causal_attention_t(q_ref, k_los, k_ref, vts, finish, tq, tk, lookahead):
    n = len(k_los)
    seq = q_ref.shape[0]
    steps = []
    for qi in range(seq // tq):
        q0 = qi * tq
        blocks = [(kb * tk, tk, 0, False) for kb in range(q0 // tk)]
        blocks += [(q0 + d * DIAG_TK, DIAG_TK, d * DIAG_TK, True) for d in range(tq // DIAG_TK)]
        steps += [(q0,) + b + (j == 0, j == len(blocks) - 1) for j, b in enumerate(blocks)]
    units = [(step, i) for step in steps for i in range(n)]
    qts = {}

    def scores(step, i):
        q0, st, klen, c_lo, masked, first, _ = step
        if first and i == 0:
            qts[q0] = [_transpose_bf16(q_ref[q0:q0 + tq, j * LANES:(j + 1) * LANES]) for j in range(n)]
        s = _dot(k_ref[st:st + klen, k_los[i]:k_los[i] + LANES], qts[q0][i][:, c_lo:])
        if masked:
            r = lax.broadcasted_iota(jnp.int32, s.shape, 0)
            c = lax.broadcasted_iota(jnp.int32, s.shape, 1)
            s = jnp.where(r <= c, s, -jnp.inf)
        return s

    state = [None] * n
    pending = [scores(*u) for u in units[:lookahead]]
    for t, (step, i) in enumerate(units):
        q0, st, klen, c_lo, masked, first, last = step
        s = pending.pop(0)
        if t + lookahead < len(units):
            pending.append(scores(*units[t + lookahead]))
        vt = vts[i][:, st:st + klen]
        m_blk = jnp.max(s, axis=0, keepdims=True)
        if first:
            m_new = m_blk
            acc = _dot(vt, jnp.exp2(s - m_new).astype(BF16))
        else:
            m, acc = state[i]
            m_new = jnp.maximum(m[:, c_lo:], m_blk)
            p = jnp.exp2(s - m_new).astype(BF16)
            acc_new = jnp.exp2(m[:, c_lo:] - m_new) * acc[:, c_lo:] + _dot(vt, p)
            if c_lo:
                m_new = jnp.concatenate([m[:, :c_lo], m_new], axis=1)
                acc_new = jnp.concatenate([acc[:, :c_lo], acc_new], axis=1)
            acc = acc_new
        state[i] = (m_new, acc)
        if last and i == n - 1:
            finish(q0, [a for _, a in state])


def _mla_attn_kernel(q_ref, k_ref, v_ref, o_ref, vt_ref):
    dv = D_MLA_V
    seq = q_ref.shape[0]
    vt = v_ref[...].astype(F32).T
    ones = jnp.ones((ONES_ROWS, seq), BF16)
    for hh in range(2):
        vt_ref[hh, :dv, :] = vt[hh * dv:(hh + 1) * dv].astype(BF16)
        vt_ref[hh, dv:, :] = ones

    def finish(q0, accs):
        ot = jnp.concatenate([a[:dv] / a[dv:dv + 1] for a in accs], axis=0)
        o_ref[q0:q0 + MLA_TQ, :] = ot.T.astype(BF16)

    _causal_attention_t(q_ref, [0, LANES], k_ref, [vt_ref.at[0], vt_ref.at[1]], finish, MLA_TQ, MLA_TK, MLA_LOOKAHEAD)


def _diff_attn_kernel(lam_ref, sg_ref, q_ref, k_ref, v_ref, o_ref, vt_ref, *, lam_init):
    dv = 2 * D_DIFF_HEAD
    seq = q_ref.shape[0]
    vt_ref[:dv, :] = _transpose_bf16(v_ref[...])
    vt_ref[dv:, :] = jnp.ones((ONES_ROWS, seq), BF16)
    lp = lam_ref[...]
    lam = (jnp.exp(jnp.sum(lp[0:1] * lp[1:2], axis=-1, keepdims=True))
           - jnp.exp(jnp.sum(lp[2:3] * lp[3:4], axis=-1, keepdims=True)) + lam_init)
    gain = sg_ref[...] * (1.0 - lam_init)

    def finish(q0, accs):
        a1, a2 = accs
        o = a1[:dv] / a1[dv:dv + 1] - lam * (a2[:dv] / a2[dv:dv + 1])
        on = o * lax.rsqrt(jnp.mean(o * o, axis=0, keepdims=True) + EPS)
        o_ref[q0:q0 + DIFF_TQ, :] = (on.T * gain).astype(BF16)

    _causal_attention_t(q_ref, [0, 0], k_ref, [vt_ref, vt_ref], finish, DIFF_TQ, DIFF_TK, DIFF_LOOKAHEAD)


def _attn_params():
    return pltpu.CompilerParams(
        dimension_semantics=("arbitrary", "arbitrary"), vmem_limit_bytes=VMEM_LIMIT)


def _mla_attn_call(qm, km, vm, batch, seq):
    t = qm.shape[0]
    pairs = N_MLA_HEADS // 2
    return pl.pallas_call(
        _mla_attn_kernel,
        grid=(batch, pairs),
        in_specs=[pl.BlockSpec((seq, 2 * LANES), lambda b, p: (b, p)),
                  pl.BlockSpec((seq, 2 * LANES), lambda b, p: (b, p)),
                  pl.BlockSpec((seq, LANES), lambda b, p: (b, p))],
        out_specs=pl.BlockSpec((seq, LANES), lambda b, p: (b, p)),
        out_shape=jax.ShapeDtypeStruct((t, N_MLA_HEADS * D_MLA_V), BF16),
        scratch_shapes=[pltpu.VMEM((2, D_MLA_V + ONES_ROWS, seq), BF16)],
        compiler_params=_attn_params(),
        name="mla_attn",
    )(qm, km, vm)


def _diff_attn_call(lam_p, sg, qd, kd, vd, batch, seq, lam_init):
    t = qd.shape[0]
    return pl.pallas_call(
        functools.partial(_diff_attn_kernel, lam_init=lam_init),
        grid=(batch, N_DIFF_HEADS),
        in_specs=[pl.BlockSpec(lam_p.shape, lambda b, j: (0, 0)),
                  pl.BlockSpec(sg.shape, lambda b, j: (0, 0)),
                  pl.BlockSpec((seq, 2 * LANES), lambda b, j: (b, j)),
                  pl.BlockSpec((seq, LANES), lambda b, j: (b, j)),
                  pl.BlockSpec((seq, LANES), lambda b, j: (b, j))],
        out_specs=pl.BlockSpec((seq, LANES), lambda b, j: (b, j)),
        out_shape=jax.ShapeDtypeStruct((t, C_DV), BF16),
        scratch_shapes=[pltpu.VMEM((2 * D_DIFF_HEAD + ONES_ROWS, seq), BF16)],
        compiler_params=_attn_params(),
        name="diff_attn",
    )(lam_p, sg, qd, kd, vd)


def _post_kernel(x_ref, om_ref, od_ref, gate_ref, wbm_ref, wbd_ref, wo_ref, gpa_ref, gpm_ref,
                 wup_ref, wdn_ref, gpo_ref, o_ref):
    subs = [slice(r0, r0 + SUB_POST) for r0 in range(0, x_ref.shape[0], SUB_POST)]
    u = [(_dot(om_ref[s, :], wbm_ref[...]), _dot(od_ref[s, :], wbd_ref[...])) for s in subs]
    y = []
    for s, (u_mla, u_diff) in zip(subs, u):
        mixed = (gate_ref[s, :D_MODEL].astype(F32) * u_mla
                 + gate_ref[s, D_MODEL:].astype(F32) * u_diff).astype(BF16)
        y.append(_dot(mixed, wo_ref[...]))
    x1 = [x_ref[s, :] + _rms(yy, gpa_ref[...]) for s, yy in zip(subs, y)]
    h = [_rms(xx, gpm_ref[...]).astype(BF16) for xx in x1]
    m = [None] * len(subs)
    for c0 in range(0, D_FF, FF_CHUNK):
        ups = [jnp.maximum(_dot(hh, wup_ref[:, c0:c0 + FF_CHUNK]), 0.0) for hh in h]
        for i, up in enumerate(ups):
            part = _dot((up * up).astype(BF16), wdn_ref[c0:c0 + FF_CHUNK, :])
            m[i] = part if m[i] is None else m[i] + part
    for s, xx, mm in zip(subs, x1, m):
        o_ref[s, :] = xx + _rms(mm, gpo_ref[...])


def _post_call(x2, om, od, gate, wbm, wbd, wo, gpa, gpm, wup, wdn, gpo):
    t = x2.shape[0]
    tm = TM_POST
    row = lambda c: pl.BlockSpec((tm, c), lambda i: (i, 0))
    return pl.pallas_call(
        _post_kernel,
        grid=(t // tm,),
        in_specs=[row(D_MODEL), row(om.shape[1]), row(od.shape[1]), row(gate.shape[1]),
                  _const_spec(wbm.shape), _const_spec(wbd.shape), _const_spec(wo.shape),
                  _const_spec((1, D_MODEL)), _const_spec((1, D_MODEL)),
                  _const_spec(wup.shape), _const_spec(wdn.shape), _const_spec((1, D_MODEL))],
        out_specs=row(D_MODEL),
        out_shape=jax.ShapeDtypeStruct((t, D_MODEL), F32),
        compiler_params=pltpu.CompilerParams(
            dimension_semantics=("arbitrary",), vmem_limit_bytes=VMEM_LIMIT),
        name="post",
    )(x2, om, od, gate, wbm, wbd, wo, gpa, gpm, wup, wdn, gpo)


def _rot_half_cols(w):
    d = w.shape[-1]
    return jnp.concatenate([-w[..., d // 2:], w[..., :d // 2]], axis=-1)


def _rope_tables(seq):
    pos = np.arange(seq, dtype=np.float64)

    def cs(d):
        inv = 1.0 / (ROPE_THETA ** (np.arange(0, d, 2, dtype=np.float64) / d))
        ang = pos[:, None] * inv[None, :]
        return np.cos(ang), np.sin(ang)

    cd, sd = cs(D_DIFF_HEAD)
    cosd = np.tile(cd, (1, 4))
    sind = np.concatenate([-sd, sd, -sd, sd], axis=1)
    cm, sm = cs(D_MLA_ROPE)
    one = np.ones((seq, D_MLA_NOPE))
    zero_n = np.zeros((seq, D_MLA_NOPE))
    zero_p = np.zeros((seq, LANES - D_MLA_NOPE - D_MLA_ROPE))
    cosm = np.concatenate([one, cm, cm, zero_p], axis=1)
    sinm = np.concatenate([zero_n, sm, sm, zero_p], axis=1)
    return tuple(jnp.asarray(t, dtype=F32) for t in (cosd, sind, cosm, sinm))


def _layout_in_proj(w_in):
    wt = jnp.swapaxes(w_in, 0, 1).astype(BF16)
    d = wt.shape[1]
    w_kr = wt[R_KR:R_KR + D_MLA_ROPE]
    half = D_MLA_ROPE // 2
    rot = jnp.concatenate([-w_kr[half:], w_kr[:half]], axis=0)
    zn = jnp.zeros((D_MLA_NOPE, d), BF16)
    zp = jnp.zeros((LANES - D_MLA_NOPE - D_MLA_ROPE, d), BF16)
    return wt, jnp.concatenate([zn, w_kr, zp, zn, rot, zp], axis=0)


def _layout_q_b(w_q_b):
    r = w_q_b.shape[0]
    w = w_q_b.reshape(r, N_MLA_HEADS, D_MLA_NOPE + D_MLA_ROPE)
    nope, rp = w[..., :D_MLA_NOPE], w[..., D_MLA_NOPE:]
    zp = jnp.zeros((r, N_MLA_HEADS, LANES - D_MLA_NOPE - D_MLA_ROPE), w.dtype)
    zn = jnp.zeros((r, N_MLA_HEADS, D_MLA_NOPE), w.dtype)
    a = jnp.concatenate([nope, rp, zp], axis=-1).reshape(r, N_MLA_HEADS * LANES)
    b = jnp.concatenate([zn, _rot_half_cols(rp), zp], axis=-1).reshape(r, N_MLA_HEADS * LANES)
    return jnp.concatenate([a, b], axis=1).astype(BF16)


def _layout_kv_b(w_kv_b):
    r = w_kv_b.shape[0]
    w = w_kv_b.reshape(r, N_MLA_HEADS, D_MLA_NOPE + D_MLA_V)
    kn, v = w[..., :D_MLA_NOPE], w[..., D_MLA_NOPE:]
    zk = jnp.zeros((r, N_MLA_HEADS, LANES - D_MLA_NOPE), w.dtype)
    wk = jnp.concatenate([kn, zk], axis=-1).reshape(r, N_MLA_HEADS * LANES)
    wv = v.reshape(r, N_MLA_HEADS * D_MLA_V)
    return jnp.concatenate([wk, wv], axis=1).astype(BF16)


def kernel(x, pre_attn_g, w_in, q_norm_g, w_q_b, kv_norm_g, w_kv_b, lambda_q1, lambda_k1, lambda_q2, lambda_k2, subln_g, w_br_mla, w_br_diff, w_out, post_attn_g, pre_mlp_g, w_mlp_up, w_mlp_down, post_mlp_g):
    batch, seq, d_model = x.shape
    depth = w_in.shape[0]
    assert d_model == D_MODEL and seq % MLA_TQ == 0 and seq % DIFF_TQ == 0 and seq % TM_PROJ == 0
    assert MLA_TQ % MLA_TK == 0 and DIFF_TQ % DIFF_TK == 0 and MLA_TQ % DIAG_TK == 0 and DIFF_TQ % DIAG_TK == 0
    assert (batch * seq) % TM_POST == 0
    cosd, sind, cosm, sinm = _rope_tables(seq)
    x2 = x.reshape(batch * seq, d_model)
    row = lambda v: v.reshape(1, -1).astype(F32)
    for l in range(depth):
        lam_init = 0.8 - 0.6 * math.exp(-0.3 * l)
        qm, km, vm, qd, kd, vd, gate = _proj_call(
            x2, row(pre_attn_g[l]), *_layout_in_proj(w_in[l]), row(q_norm_g[l]),
            _layout_q_b(w_q_b[l]), row(kv_norm_g[l]), _layout_kv_b(w_kv_b[l]),
            cosd, sind, cosm, sinm, seq)
        om = _mla_attn_call(qm, km, vm, batch, seq)
        lam_p = jnp.stack([lambda_q1[l], lambda_k1[l], lambda_q2[l], lambda_k2[l]]).astype(F32)
        od = _diff_attn_call(lam_p, row(subln_g[l]), qd, kd, vd, batch, seq, lam_init)
        x2 = _post_call(
            x2, om, od, gate, w_br_mla[l].astype(BF16), w_br_diff[l].astype(BF16),
            w_out[l].astype(BF16), row(post_attn_g[l]), row(pre_mlp_g[l]),
            w_mlp_up[l].astype(BF16), w_mlp_down[l].astype(BF16), row(post_mlp_g[l]))
    return x2.reshape(batch, seq, d_model)
```

```python
import functools
import math

import numpy as np
import jax
import jax.numpy as jnp
from jax import lax
from jax.experimental import pallas as pl
from jax.experimental.pallas import tpu as pltpu

D_MODEL = 1024
N_MLA_HEADS = 8
D_MLA_NOPE = 64
D_MLA_ROPE = 32
D_MLA_V = 64
Q_LORA = 256
KV_LORA = 128
N_DIFF_HEADS = 8
D_DIFF_HEAD = 64
D_FF = 4 * D_MODEL
ROPE_THETA = 10000.0
EPS = 1e-6

LANES = 128
VMEM_LIMIT = 52 * 1024 * 1024

C_QA = Q_LORA
C_KVA = KV_LORA + D_MLA_ROPE
C_DQ = 2 * N_DIFF_HEADS * D_DIFF_HEAD
C_DK = C_DQ
C_DV = N_DIFF_HEADS * 2 * D_DIFF_HEAD
C_GATE = 2 * D_MODEL

R_QA = 0
R_CKV = R_QA + Q_LORA
R_KR = R_CKV + KV_LORA
R_DQ = R_KR + D_MLA_ROPE
R_DK = R_DQ + C_DQ
R_DV = R_DK + C_DK
R_GATE = R_DV + C_DV
D_IN_TOTAL = R_GATE + C_GATE

TM_PROJ = 512
SUB_PROJ = 256
TM_POST = 512
SUB_POST = 256
FF_CHUNK = 1024
MLA_TQ, MLA_TK, MLA_LOOKAHEAD = 512, 256, 5
DIFF_TQ, DIFF_TK, DIFF_LOOKAHEAD = 1024, 512, 2
MLA_HEADS_PER_STEP = 4
DIFF_HEADS_PER_STEP = 2
DIAG_TK = 256

LOG2E = math.log2(math.e)

F32 = jnp.float32
BF16 = jnp.bfloat16


def _rms(x, g):
    return x * lax.rsqrt(jnp.mean(x * x, axis=-1, keepdims=True) + EPS) * g


def _dot(a, b):
    return jnp.dot(a, b, preferred_element_type=F32)


def _dot_nt(a, b):
    return lax.dot_general(a, b, (((1,), (1,)), ((), ())), preferred_element_type=F32)


def _proj_kernel(x_ref, g_ref, wt_ref, krt_ref, qng_ref, wqb_ref, kvg_ref, wkvb_ref,
                 cosd_ref, sind_ref, cosm_ref, sinm_ref,
                 qm_ref, km_ref, vm_ref, qd_ref, kd_ref, vd_ref, gate_ref,
                 *, mla_scale, diff_scale):
    subs = [slice(r0, r0 + SUB_PROJ) for r0 in range(0, x_ref.shape[0], SUB_PROJ)]
    half = N_MLA_HEADS * LANES
    lane = lax.broadcasted_iota(jnp.int32, (1, LANES), 1)
    low_half = lane % D_DIFF_HEAD < D_DIFF_HEAD // 2
    mask_a = jnp.where(lane < D_DIFF_HEAD, diff_scale, 0.0).astype(F32)
    mask_b = jnp.where(lane < D_DIFF_HEAD, 0.0, diff_scale).astype(F32)

    def seg(hh, r0, n):
        return _dot_nt(hh, wt_ref[r0:r0 + n, :])

    def rope_group(xg, cosd, sind):
        up = pltpu.roll(xg, LANES - D_DIFF_HEAD // 2, 1)
        down = pltpu.roll(xg, D_DIFF_HEAD // 2, 1)
        return xg * cosd + jnp.where(low_half, up, down) * sind

    h = [_rms(x_ref[s, :], g_ref[...]).astype(BF16) for s in subs]
    lat = [(seg(hh, R_QA, Q_LORA), seg(hh, R_CKV, KV_LORA),
            _dot_nt(hh, krt_ref[...]), seg(hh, R_GATE, C_GATE)) for hh in h]

    up_proj = []
    for s, hh, (qa, ckv, kr, gl) in zip(subs, h, lat):
        qn = _rms(qa, qng_ref[...]).astype(BF16)
        kn = _rms(ckv, kvg_ref[...]).astype(BF16)
        krp = kr[:, :LANES] * cosm_ref[s, :] + kr[:, LANES:] * sinm_ref[s, :]
        up_proj.append((_dot(qn, wqb_ref[...]), _dot(kn, wkvb_ref[...]), krp, seg(hh, R_DQ, C_DQ)))
        gate_ref[s, :] = (1.0 / (1.0 + jnp.exp(-gl))).astype(BF16)

    dks = []
    for s, hh, (qab, kvb, krp, dq) in zip(subs, h, up_proj):
        dks.append(seg(hh, R_DK, C_DK))
        cosm = cosm_ref[s, :]
        sinm = sinm_ref[s, :]
        for j in range(N_MLA_HEADS):
            a = qab[:, j * LANES:(j + 1) * LANES]
            b = qab[:, half + j * LANES:half + (j + 1) * LANES]
            qm_ref[s, j * LANES:(j + 1) * LANES] = ((a * cosm + b * sinm) * mla_scale).astype(BF16)
        for j in range(N_MLA_HEADS):
            km_ref[s, j * LANES:(j + 1) * LANES] = (kvb[:, j * LANES:(j + 1) * LANES] + krp).astype(BF16)
        vm_ref[s, :] = kvb[:, half:].astype(BF16)
        for j in range(N_DIFF_HEADS):
            r = rope_group(dq[:, j * LANES:(j + 1) * LANES], cosd_ref[s, :], sind_ref[s, :])
            qd_ref[s, 2 * j * LANES:(2 * j + 1) * LANES] = (r * mask_a).astype(BF16)
            qd_ref[s, (2 * j + 1) * LANES:(2 * j + 2) * LANES] = (r * mask_b).astype(BF16)

    for s, hh, dk in zip(subs, h, dks):
        dv = seg(hh, R_DV, C_DV)
        for j in range(N_DIFF_HEADS):
            kd_ref[s, j * LANES:(j + 1) * LANES] = rope_group(
                dk[:, j * LANES:(j + 1) * LANES], cosd_ref[s, :], sind_ref[s, :]).astype(BF16)
        vd_ref[s, :] = dv.astype(BF16)


def _const_spec(shape):
    return pl.BlockSpec(shape, lambda i: (0,) * len(shape), pipeline_mode=pl.Buffered(1))


def _proj_call(x2, g, wt, krt, qng, wqb, kvg, wkvb, cosd, sind, cosm, sinm, seq):
    t = x2.shape[0]
    tm = TM_PROJ
    tiles_per_seq = seq // tm
    row = lambda c: pl.BlockSpec((tm, c), lambda i: (i, 0))
    tab = pl.BlockSpec((tm, LANES), lambda i: (i % tiles_per_seq, 0))
    widths = (N_MLA_HEADS * LANES, N_MLA_HEADS * LANES, N_MLA_HEADS * D_MLA_V,
              2 * C_DQ, C_DK, C_DV, C_GATE)
    kern = functools.partial(
        _proj_kernel,
        mla_scale=LOG2E / math.sqrt(D_MLA_NOPE + D_MLA_ROPE),
        diff_scale=LOG2E / math.sqrt(D_DIFF_HEAD))
    return pl.pallas_call(
        kern,
        grid=(t // tm,),
        in_specs=[row(D_MODEL), _const_spec((1, D_MODEL)), _const_spec(wt.shape), _const_spec(krt.shape),
                  _const_spec((1, Q_LORA)), _const_spec(wqb.shape),
                  _const_spec((1, KV_LORA)), _const_spec(wkvb.shape),
                  tab, tab, tab, tab],
        out_specs=[row(c) for c in widths],
        out_shape=[jax.ShapeDtypeStruct((t, c), BF16) for c in widths],
        compiler_params=pltpu.CompilerParams(
            dimension_semantics=("arbitrary",), vmem_limit_bytes=VMEM_LIMIT),
        name="proj",
    )(x2, g, wt, krt, qng, wqb, kvg, wkvb, cosd, sind, cosm, sinm)


ONES_ROWS = 16


def _transpose_bf16(x):
    return x.astype(F32).T.astype(BF16)


def _causal_attention_t(q_ref, k_los, k_ref, vts, finish, tq, tk, lookahead):
    n = len(k_los)
    seq = q_ref.shape[0]
    steps = []
    for qi in range(seq // tq):
        q0 = qi * tq
        blocks = [(kb * tk, tk, 0, False) for kb in range(q0 // tk)]
        blocks += [(q0 + d * DIAG_TK, DIAG_TK, d * DIAG_TK, True) for d in range(tq // DIAG_TK)]
        steps += [(q0,) + b + (j == 0, j == len(blocks) - 1) for j, b in enumerate(blocks)]
    units = [(step, i) for step in steps for i in range(n)]
    qts = {}

    def scores(step, i):
        q0, st, klen, c_lo, masked, first, _ = step
        if first and i == 0:
            qts[q0] = [_transpose_bf16(q_ref[q0:q0 + tq, j * LANES:(j + 1) * LANES]) for j in range(n)]
        s = _dot(k_ref[st:st + klen, k_los[i]:k_los[i] + LANES], qts[q0][i][:, c_lo:])
        if masked:
            r = lax.broadcasted_iota(jnp.int32, s.shape, 0)
            c = lax.broadcasted_iota(jnp.int32, s.shape, 1)
            s = jnp.where(r <= c, s, -jnp.inf)
        return s

    state = [None] * n
    pending = [scores(*u) for u in units[:lookahead]]
    for t, (step, i) in enumerate(units):
        q0, st, klen, c_lo, masked, first, last = step
        s = pending.pop(0)
        if t + lookahead < len(units):
            pending.append(scores(*units[t + lookahead]))
        vt = vts[i][:, st:st + klen]
        m_blk = jnp.max(s, axis=0, keepdims=True)
        if first:
            m_new = m_blk
            acc = _dot(vt, jnp.exp2(s - m_new).astype(BF16))
        else:
            m, acc = state[i]
            m_new = jnp.maximum(m[:, c_lo:], m_blk)
            p = jnp.exp2(s - m_new).astype(BF16)
            acc_new = jnp.exp2(m[:, c_lo:] - m_new) * acc[:, c_lo:] + _dot(vt, p)
            if c_lo:
                m_new = jnp.concatenate([m[:, :c_lo], m_new], axis=1)
                acc_new = jnp.concatenate([acc[:, :c_lo], acc_new], axis=1)
            acc = acc_new
        state[i] = (m_new, acc)
        if last and i == n - 1:
            finish(q0, [a for _, a in state])


def _mla_attn_kernel(q_ref, k_ref, v_ref, o_ref, vt_ref):
    dv = D_MLA_V
    seq = q_ref.shape[0]
    nh = MLA_HEADS_PER_STEP
    vt = v_ref[...].astype(F32).T
    ones = jnp.ones((ONES_ROWS, seq), BF16)
    for hh in range(nh):
        vt_ref[hh, :dv, :] = vt[hh * dv:(hh + 1) * dv].astype(BF16)
        vt_ref[hh, dv:, :] = ones

    def finish(q0, accs):
        ot = jnp.concatenate([a[:dv] / a[dv:dv + 1] for a in accs], axis=0)
        o_ref[q0:q0 + MLA_TQ, :] = ot.T.astype(BF16)

    _causal_attention_t(q_ref, [hh * LANES for hh in range(nh)], k_ref, [vt_ref.at[hh] for hh in range(nh)],
                        finish, MLA_TQ, MLA_TK, MLA_LOOKAHEAD)


def _diff_attn_kernel(lam_ref, sg_ref, q_ref, k_ref, v_ref, o_ref, vt_ref, *, lam_init):
    dv = 2 * D_DIFF_HEAD
    seq = q_ref.shape[0]
    nh = DIFF_HEADS_PER_STEP
    for hh in range(nh):
        vt_ref[hh, :dv, :] = _transpose_bf16(v_ref[:, hh * LANES:(hh + 1) * LANES])
        vt_ref[hh, dv:, :] = jnp.ones((ONES_ROWS, seq), BF16)
    lp = lam_ref[...]
    lam = (jnp.exp(jnp.sum(lp[0:1] * lp[1:2], axis=-1, keepdims=True))
           - jnp.exp(jnp.sum(lp[2:3] * lp[3:4], axis=-1, keepdims=True)) + lam_init)
    gain = sg_ref[...] * (1.0 - lam_init)

    def finish(q0, accs):
        for hh in range(nh):
            a1, a2 = accs[2 * hh], accs[2 * hh + 1]
            o = a1[:dv] / a1[dv:dv + 1] - lam * (a2[:dv] / a2[dv:dv + 1])
            on = o * lax.rsqrt(jnp.mean(o * o, axis=0, keepdims=True) + EPS)
            o_ref[q0:q0 + DIFF_TQ, hh * LANES:(hh + 1) * LANES] = (on.T * gain).astype(BF16)

    k_los = [hh * LANES for hh in range(nh) for _ in range(2)]
    vts = [vt_ref.at[hh] for hh in range(nh) for _ in range(2)]
    _causal_attention_t(q_ref, k_los, k_ref, vts, finish, DIFF_TQ, DIFF_TK, DIFF_LOOKAHEAD)


def _attn_params():
    return pltpu.CompilerParams(
        dimension_semantics=("arbitrary", "arbitrary"), vmem_limit_bytes=VMEM_LIMIT)


def _mla_attn_call(qm, km, vm, batch, seq):
    t = qm.shape[0]
    nh = MLA_HEADS_PER_STEP
    return pl.pallas_call(
        _mla_attn_kernel,
        grid=(batch, N_MLA_HEADS // nh),
        in_specs=[pl.BlockSpec((seq, nh * LANES), lambda b, p: (b, p)),
                  pl.BlockSpec((seq, nh * LANES), lambda b, p: (b, p)),
                  pl.BlockSpec((seq, nh * D_MLA_V), lambda b, p: (b, p))],
        out_specs=pl.BlockSpec((seq, nh * D_MLA_V), lambda b, p: (b, p)),
        out_shape=jax.ShapeDtypeStruct((t, N_MLA_HEADS * D_MLA_V), BF16),
        scratch_shapes=[pltpu.VMEM((nh, D_MLA_V + ONES_ROWS, seq), BF16)],
        compiler_params=_attn_params(),
        name="mla_attn",
    )(qm, km, vm)


def _diff_attn_call(lam_p, sg, qd, kd, vd, batch, seq, lam_init):
    t = qd.shape[0]
    nh = DIFF_HEADS_PER_STEP
    return pl.pallas_call(
        functools.partial(_diff_attn_kernel, lam_init=lam_init),
        grid=(batch, N_DIFF_HEADS // nh),
        in_specs=[pl.BlockSpec(lam_p.shape, lambda b, j: (0, 0)),
                  pl.BlockSpec(sg.shape, lambda b, j: (0, 0)),
                  pl.BlockSpec((seq, 2 * nh * LANES), lambda b, j: (b, j)),
                  pl.BlockSpec((seq, nh * LANES), lambda b, j: (b, j)),
                  pl.BlockSpec((seq, nh * LANES), lambda b, j: (b, j))],
        out_specs=pl.BlockSpec((seq, nh * LANES), lambda b, j: (b, j)),
        out_shape=jax.ShapeDtypeStruct((t, C_DV), BF16),
        scratch_shapes=[pltpu.VMEM((nh, 2 * D_DIFF_HEAD + ONES_ROWS, seq), BF16)],
        compiler_params=_attn_params(),
        name="diff_attn",
    )(lam_p, sg, qd, kd, vd)


def _post_kernel(x_ref, om_ref, od_ref, gate_ref, wbm_ref, wbd_ref, wo_ref, gpa_ref, gpm_ref,
                 wup_ref, wdn_ref, gpo_ref, o_ref):
    subs = [slice(r0, r0 + SUB_POST) for r0 in range(0, x_ref.shape[0], SUB_POST)]
    u = [(_dot(om_ref[s, :], wbm_ref[...]), _dot(od_ref[s, :], wbd_ref[...])) for s in subs]
    y = []
    for s, (u_mla, u_diff) in zip(subs, u):
        mixed = (gate_ref[s, :D_MODEL].astype(F32) * u_mla
                 + gate_ref[s, D_MODEL:].astype(F32) * u_diff).astype(BF16)
        y.append(_dot(mixed, wo_ref[...]))
    x1 = [x_ref[s, :] + _rms(yy, gpa_ref[...]) for s, yy in zip(subs, y)]
    h = [_rms(xx, gpm_ref[...]).astype(BF16) for xx in x1]
    m = [None] * len(subs)
    for c0 in range(0, D_FF, FF_CHUNK):
        ups = [jnp.maximum(_dot(hh, wup_ref[:, c0:c0 + FF_CHUNK]), 0.0) for hh in h]
        for i, up in enumerate(ups):
            part = _dot((up * up).astype(BF16), wdn_ref[c0:c0 + FF_CHUNK, :])
            m[i] = part if m[i] is None else m[i] + part
    for s, xx, mm in zip(subs, x1, m):
        o_ref[s, :] = xx + _rms(mm, gpo_ref[...])


def _post_call(x2, om, od, gate, wbm, wbd, wo, gpa, gpm, wup, wdn, gpo):
    t = x2.shape[0]
    tm = TM_POST
    row = lambda c: pl.BlockSpec((tm, c), lambda i: (i, 0))
    return pl.pallas_call(
        _post_kernel,
        grid=(t // tm,),
        in_specs=[row(D_MODEL), row(om.shape[1]), row(od.shape[1]), row(gate.shape[1]),
                  _const_spec(wbm.shape), _const_spec(wbd.shape), _const_spec(wo.shape),
                  _const_spec((1, D_MODEL)), _const_spec((1, D_MODEL)),
                  _const_spec(wup.shape), _const_spec(wdn.shape), _const_spec((1, D_MODEL))],
        out_specs=row(D_MODEL),
        out_shape=jax.ShapeDtypeStruct((t, D_MODEL), F32),
        compiler_params=pltpu.CompilerParams(
            dimension_semantics=("arbitrary",), vmem_limit_bytes=VMEM_LIMIT),
        name="post",
    )(x2, om, od, gate, wbm, wbd, wo, gpa, gpm, wup, wdn, gpo)


def _rot_half_cols(w):
    d = w.shape[-1]
    return jnp.concatenate([-w[..., d // 2:], w[..., :d // 2]], axis=-1)


def _rope_tables(seq):
    pos = np.arange(seq, dtype=np.float64)

    def cs(d):
        inv = 1.0 / (ROPE_THETA ** (np.arange(0, d, 2, dtype=np.float64) / d))
        ang = pos[:, None] * inv[None, :]
        return np.cos(ang), np.sin(ang)

    cd, sd = cs(D_DIFF_HEAD)
    cosd = np.tile(cd, (1, 4))
    sind = np.concatenate([-sd, sd, -sd, sd], axis=1)
    cm, sm = cs(D_MLA_ROPE)
    one = np.ones((seq, D_MLA_NOPE))
    zero_n = np.zeros((seq, D_MLA_NOPE))
    zero_p = np.zeros((seq, LANES - D_MLA_NOPE - D_MLA_ROPE))
    cosm = np.concatenate([one, cm, cm, zero_p], axis=1)
    sinm = np.concatenate([zero_n, sm, sm, zero_p], axis=1)
    return tuple(jnp.asarray(t, dtype=F32) for t in (cosd, sind, cosm, sinm))


def _layout_in_proj(w_in):
    wt = jnp.swapaxes(w_in, 0, 1).astype(BF16)
    d = wt.shape[1]
    w_kr = wt[R_KR:R_KR + D_MLA_ROPE]
    half = D_MLA_ROPE // 2
    rot = jnp.concatenate([-w_kr[half:], w_kr[:half]], axis=0)
    zn = jnp.zeros((D_MLA_NOPE, d), BF16)
    zp = jnp.zeros((LANES - D_MLA_NOPE - D_MLA_ROPE, d), BF16)
    return wt, jnp.concatenate([zn, w_kr, zp, zn, rot, zp], axis=0)


def _layout_q_b(w_q_b):
    r = w_q_b.shape[0]
    w = w_q_b.reshape(r, N_MLA_HEADS, D_MLA_NOPE + D_MLA_ROPE)
    nope, rp = w[..., :D_MLA_NOPE], w[..., D_MLA_NOPE:]
    zp = jnp.zeros((r, N_MLA_HEADS, LANES - D_MLA_NOPE - D_MLA_ROPE), w.dtype)
    zn = jnp.zeros((r, N_MLA_HEADS, D_MLA_NOPE), w.dtype)
    a = jnp.concatenate([nope, rp, zp], axis=-1).reshape(r, N_MLA_HEADS * LANES)
    b = jnp.concatenate([zn, _rot_half_cols(rp), zp], axis=-1).reshape(r, N_MLA_HEADS * LANES)
    return jnp.concatenate([a, b], axis=1).astype(BF16)


def _layout_kv_b(w_kv_b):
    r = w_kv_b.shape[0]
    w = w_kv_b.reshape(r, N_MLA_HEADS, D_MLA_NOPE + D_MLA_V)
    kn, v = w[..., :D_MLA_NOPE], w[..., D_MLA_NOPE:]
    zk = jnp.zeros((r, N_MLA_HEADS, LANES - D_MLA_NOPE), w.dtype)
    wk = jnp.concatenate([kn, zk], axis=-1).reshape(r, N_MLA_HEADS * LANES)
    wv = v.reshape(r, N_MLA_HEADS * D_MLA_V)
    return jnp.concatenate([wk, wv], axis=1).astype(BF16)


def kernel(x, pre_attn_g, w_in, q_norm_g, w_q_b, kv_norm_g, w_kv_b, lambda_q1, lambda_k1, lambda_q2, lambda_k2, subln_g, w_br_mla, w_br_diff, w_out, post_attn_g, pre_mlp_g, w_mlp_up, w_mlp_down, post_mlp_g):
    batch, seq, d_model = x.shape
    depth = w_in.shape[0]
    assert d_model == D_MODEL and seq % MLA_TQ == 0 and seq % DIFF_TQ == 0 and seq % TM_PROJ == 0
    assert MLA_TQ % MLA_TK == 0 and DIFF_TQ % DIFF_TK == 0 and MLA_TQ % DIAG_TK == 0 and DIFF_TQ % DIAG_TK == 0
    assert (batch * seq) % TM_POST == 0
    cosd, sind, cosm, sinm = _rope_tables(seq)
    x2 = x.reshape(batch * seq, d_model)
    row = lambda v: v.reshape(1, -1).astype(F32)
    for l in range(depth):
        lam_init = 0.8 - 0.6 * math.exp(-0.3 * l)
        qm, km, vm, qd, kd, vd, gate = _proj_call(
            x2, row(pre_attn_g[l]), *_layout_in_proj(w_in[l]), row(q_norm_g[l]),
            _layout_q_b(w_q_b[l]), row(kv_norm_g[l]), _layout_kv_b(w_kv_b[l]),
            cosd, sind, cosm, sinm, seq)
        om = _mla_attn_call(qm, km, vm, batch, seq)
        lam_p = jnp.stack([lambda_q1[l], lambda_k1[l], lambda_q2[l], lambda_k2[l]]).astype(F32)
        od = _diff_attn_call(lam_p, row(subln_g[l]), qd, kd, vd, batch, seq, lam_init)
        x2 = _post_call(
            x2, om, od, gate, w_br_mla[l].astype(BF16), w_br_diff[l].astype(BF16),
            w_out[l].astype(BF16), row(post_attn_g[l]), row(pre_mlp_g[l]),
            w_mlp_up[l].astype(BF16), w_mlp_down[l].astype(BF16), row(post_mlp_g[l]))
    return x2.reshape(batch, seq, d_model)
```

```python
import functools
import math

import numpy as np
import jax
import jax.numpy as jnp
from jax import lax
from jax.experimental import pallas as pl
from jax.experimental.pallas import tpu as pltpu

D_MODEL = 1024
N_MLA_HEADS = 8
D_MLA_NOPE = 64
D_MLA_ROPE = 32
D_MLA_V = 64
Q_LORA = 256
KV_LORA = 128
N_DIFF_HEADS = 8
D_DIFF_HEAD = 64
D_FF = 4 * D_MODEL
ROPE_THETA = 10000.0
EPS = 1e-6

LANES = 128
VMEM_LIMIT = 52 * 1024 * 1024

C_QA = Q_LORA
C_KVA = KV_LORA + D_MLA_ROPE
C_DQ = 2 * N_DIFF_HEADS * D_DIFF_HEAD
C_DK = C_DQ
C_DV = N_DIFF_HEADS * 2 * D_DIFF_HEAD
C_GATE = 2 * D_MODEL

R_QA = 0
R_CKV = R_QA + Q_LORA
R_KR = R_CKV + KV_LORA
R_DQ = R_KR + D_MLA_ROPE
R_DK = R_DQ + C_DQ
R_DV = R_DK + C_DK
R_GATE = R_DV + C_DV
D_IN_TOTAL = R_GATE + C_GATE

TM_PROJ = 512
SUB_PROJ = 256
TM_POST = 512
SUB_POST = 256
FF_CHUNK = 1024
MLA_TQ, MLA_TK, MLA_LOOKAHEAD = 512, 256, 5
DIFF_TQ, DIFF_TK, DIFF_LOOKAHEAD = 1024, 512, 2
MLA_HEADS_PER_STEP = 4
DIFF_HEADS_PER_STEP = 2
DIAG_TK = 256

LOG2E = math.log2(math.e)

F32 = jnp.float32
BF16 = jnp.bfloat16


def _rms(x, g):
    return x * lax.rsqrt(jnp.mean(x * x, axis=-1, keepdims=True) + EPS) * g


def _dot(a, b):
    return jnp.dot(a, b, preferred_element_type=F32)


def _dot_nt(a, b):
    return lax.dot_general(a, b, (((1,), (1,)), ((), ())), preferred_element_type=F32)


def _proj_kernel(x_ref, g_ref, wt_ref, krt_ref, qng_ref, wqb_ref, kvg_ref, wkvb_ref,
                 cosd_ref, sind_ref, cosm_ref, sinm_ref,
                 qm_ref, km_ref, vm_ref, qd_ref, kd_ref, vd_ref, gate_ref,
                 *, mla_scale, diff_scale):
    subs = [slice(r0, r0 + SUB_PROJ) for r0 in range(0, x_ref.shape[0], SUB_PROJ)]
    half = N_MLA_HEADS * LANES
    lane = lax.broadcasted_iota(jnp.int32, (1, LANES), 1)
    low_half = lane % D_DIFF_HEAD < D_DIFF_HEAD // 2

    def seg(hh, r0, n):
        return _dot_nt(hh, wt_ref[r0:r0 + n, :])

    def rope_group(xg, cosd, sind):
        up = pltpu.roll(xg, LANES - D_DIFF_HEAD // 2, 1)
        down = pltpu.roll(xg, D_DIFF_HEAD // 2, 1)
        return xg * cosd + jnp.where(low_half, up, down) * sind

    h = [_rms(x_ref[s, :], g_ref[...]).astype(BF16) for s in subs]
    lat = [(seg(hh, R_QA, Q_LORA), seg(hh, R_CKV, KV_LORA),
            _dot_nt(hh, krt_ref[...]), seg(hh, R_GATE, C_GATE)) for hh in h]

    up_proj = []
    for s, hh, (qa, ckv, kr, gl) in zip(subs, h, lat):
        qn = _rms(qa, qng_ref[...]).astype(BF16)
        kn = _rms(ckv, kvg_ref[...]).astype(BF16)
        krp = kr[:, :LANES] * cosm_ref[s, :] + kr[:, LANES:] * sinm_ref[s, :]
        up_proj.append((_dot(qn, wqb_ref[...]), _dot(kn, wkvb_ref[...]), krp, seg(hh, R_DQ, C_DQ)))
        gate_ref[s, :] = (1.0 / (1.0 + jnp.exp(-gl))).astype(BF16)

    dks = []
    for s, hh, (qab, kvb, krp, dq) in zip(subs, h, up_proj):
        dks.append(seg(hh, R_DK, C_DK))
        cosm = cosm_ref[s, :]
        sinm = sinm_ref[s, :]
        for j in range(N_MLA_HEADS):
            a = qab[:, j * LANES:(j + 1) * LANES]
            b = qab[:, half + j * LANES:half + (j + 1) * LANES]
            qm_ref[s, j * LANES:(j + 1) * LANES] = ((a * cosm + b * sinm) * mla_scale).astype(BF16)
        for j in range(N_MLA_HEADS):
            km_ref[s, j * LANES:(j + 1) * LANES] = (kvb[:, j * LANES:(j + 1) * LANES] + krp).astype(BF16)
        vm_ref[s, :] = kvb[:, half:].astype(BF16)
        for j in range(N_DIFF_HEADS):
            r = rope_group(dq[:, j * LANES:(j + 1) * LANES], cosd_ref[s, :], sind_ref[s, :])
            qd_ref[s, j * LANES:(j + 1) * LANES] = (r * diff_scale).astype(BF16)

    for s, hh, dk in zip(subs, h, dks):
        dv = seg(hh, R_DV, C_DV)
        for j in range(N_DIFF_HEADS):
            kd_ref[s, j * LANES:(j + 1) * LANES] = rope_group(
                dk[:, j * LANES:(j + 1) * LANES], cosd_ref[s, :], sind_ref[s, :]).astype(BF16)
        vd_ref[s, :] = dv.astype(BF16)


def _const_spec(shape):
    return pl.BlockSpec(shape, lambda i: (0,) * len(shape), pipeline_mode=pl.Buffered(1))


def _proj_call(x2, g, wt, krt, qng, wqb, kvg, wkvb, cosd, sind, cosm, sinm, seq):
    t = x2.shape[0]
    tm = TM_PROJ
    tiles_per_seq = seq // tm
    row = lambda c: pl.BlockSpec((tm, c), lambda i: (i, 0))
    tab = pl.BlockSpec((tm, LANES), lambda i: (i % tiles_per_seq, 0))
    widths = (N_MLA_HEADS * LANES, N_MLA_HEADS * LANES, N_MLA_HEADS * D_MLA_V,
              C_DQ, C_DK, C_DV, C_GATE)
    kern = functools.partial(
        _proj_kernel,
        mla_scale=LOG2E / math.sqrt(D_MLA_NOPE + D_MLA_ROPE),
        diff_scale=LOG2E / math.sqrt(D_DIFF_HEAD))
    return pl.pallas_call(
        kern,
        grid=(t // tm,),
        in_specs=[row(D_MODEL), _const_spec((1, D_MODEL)), _const_spec(wt.shape), _const_spec(krt.shape),
                  _const_spec((1, Q_LORA)), _const_spec(wqb.shape),
                  _const_spec((1, KV_LORA)), _const_spec(wkvb.shape),
                  tab, tab, tab, tab],
        out_specs=[row(c) for c in widths],
        out_shape=[jax.ShapeDtypeStruct((t, c), BF16) for c in widths],
        compiler_params=pltpu.CompilerParams(
            dimension_semantics=("arbitrary",), vmem_limit_bytes=VMEM_LIMIT),
        name="proj",
    )(x2, g, wt, krt, qng, wqb, kvg, wkvb, cosd, sind, cosm, sinm)


ONES_ROWS = 16


def _transpose_bf16(x):
    return x.astype(F32).T.astype(BF16)


def _causal_attention_t(load_qts, seq, k_los, k_ref, vts, finish, tq, tk, lookahead):
    n = len(k_los)
    steps = []
    for qi in range(seq // tq):
        q0 = qi * tq
        blocks = [(kb * tk, tk, 0, False) for kb in range(q0 // tk)]
        blocks += [(q0 + d * DIAG_TK, DIAG_TK, d * DIAG_TK, True) for d in range(tq // DIAG_TK)]
        steps += [(q0,) + b + (j == 0, j == len(blocks) - 1) for j, b in enumerate(blocks)]
    units = [(step, i) for step in steps for i in range(n)]
    qts = {}

    def scores(step, i):
        q0, st, klen, c_lo, masked, first, _ = step
        if first and i == 0:
            qts[q0] = load_qts(q0, tq)
        s = _dot(k_ref[st:st + klen, k_los[i]:k_los[i] + LANES], qts[q0][i][:, c_lo:])
        if masked:
            r = lax.broadcasted_iota(jnp.int32, s.shape, 0)
            c = lax.broadcasted_iota(jnp.int32, s.shape, 1)
            s = jnp.where(r <= c, s, -jnp.inf)
        return s

    state = [None] * n
    pending = [scores(*u) for u in units[:lookahead]]
    for t, (step, i) in enumerate(units):
        q0, st, klen, c_lo, masked, first, last = step
        s = pending.pop(0)
        if t + lookahead < len(units):
            pending.append(scores(*units[t + lookahead]))
        vt = vts[i][:, st:st + klen]
        m_blk = jnp.max(s, axis=0, keepdims=True)
        if first:
            m_new = m_blk
            acc = _dot(vt, jnp.exp2(s - m_new).astype(BF16))
        else:
            m, acc = state[i]
            m_new = jnp.maximum(m[:, c_lo:], m_blk)
            p = jnp.exp2(s - m_new).astype(BF16)
            acc_new = jnp.exp2(m[:, c_lo:] - m_new) * acc[:, c_lo:] + _dot(vt, p)
            if c_lo:
                m_new = jnp.concatenate([m[:, :c_lo], m_new], axis=1)
                acc_new = jnp.concatenate([acc[:, :c_lo], acc_new], axis=1)
            acc = acc_new
        state[i] = (m_new, acc)
        if last and i == n - 1:
            finish(q0, [a for _, a in state])


def _mla_attn_kernel(q_ref, k_ref, v_ref, o_ref, vt_ref):
    dv = D_MLA_V
    seq = q_ref.shape[0]
    nh = MLA_HEADS_PER_STEP
    vt = v_ref[...].astype(F32).T
    ones = jnp.ones((ONES_ROWS, seq), BF16)
    for hh in range(nh):
        vt_ref[hh, :dv, :] = vt[hh * dv:(hh + 1) * dv].astype(BF16)
        vt_ref[hh, dv:, :] = ones

    def finish(q0, accs):
        ot = jnp.concatenate([a[:dv] / a[dv:dv + 1] for a in accs], axis=0)
        o_ref[q0:q0 + MLA_TQ, :] = ot.T.astype(BF16)

    def load_qts(q0, tq):
        return [_transpose_bf16(q_ref[q0:q0 + tq, hh * LANES:(hh + 1) * LANES]) for hh in range(nh)]

    _causal_attention_t(load_qts, seq, [hh * LANES for hh in range(nh)], k_ref,
                        [vt_ref.at[hh] for hh in range(nh)], finish, MLA_TQ, MLA_TK, MLA_LOOKAHEAD)


def _diff_attn_kernel(lam_ref, sg_ref, q_ref, k_ref, v_ref, o_ref, vt_ref, *, lam_init):
    dv = 2 * D_DIFF_HEAD
    seq = q_ref.shape[0]
    nh = DIFF_HEADS_PER_STEP
    for hh in range(nh):
        vt_ref[hh, :dv, :] = _transpose_bf16(v_ref[:, hh * LANES:(hh + 1) * LANES])
        vt_ref[hh, dv:, :] = jnp.ones((ONES_ROWS, seq), BF16)
    lp = lam_ref[...]
    lam = (jnp.exp(jnp.sum(lp[0:1] * lp[1:2], axis=-1, keepdims=True))
           - jnp.exp(jnp.sum(lp[2:3] * lp[3:4], axis=-1, keepdims=True)) + lam_init)
    gain = sg_ref[...] * (1.0 - lam_init)

    def finish(q0, accs):
        for hh in range(nh):
            a1, a2 = accs[2 * hh], accs[2 * hh + 1]
            o = a1[:dv] / a1[dv:dv + 1] - lam * (a2[:dv] / a2[dv:dv + 1])
            on = o * lax.rsqrt(jnp.mean(o * o, axis=0, keepdims=True) + EPS)
            o_ref[q0:q0 + DIFF_TQ, hh * LANES:(hh + 1) * LANES] = (on.T * gain).astype(BF16)

    def load_qts(q0, tq):
        out = []
        zero = jnp.zeros((D_DIFF_HEAD, tq), BF16)
        for hh in range(nh):
            qt = _transpose_bf16(q_ref[q0:q0 + tq, hh * LANES:(hh + 1) * LANES])
            out += [jnp.concatenate([qt[:D_DIFF_HEAD], zero], axis=0),
                    jnp.concatenate([zero, qt[D_DIFF_HEAD:]], axis=0)]
        return out

    k_los = [hh * LANES for hh in range(nh) for _ in range(2)]
    vts = [vt_ref.at[hh] for hh in range(nh) for _ in range(2)]
    _causal_attention_t(load_qts, seq, k_los, k_ref, vts, finish, DIFF_TQ, DIFF_TK, DIFF_LOOKAHEAD)


def _attn_params():
    return pltpu.CompilerParams(
        dimension_semantics=("arbitrary", "arbitrary"), vmem_limit_bytes=VMEM_LIMIT)


def _mla_attn_call(qm, km, vm, batch, seq):
    t = qm.shape[0]
    nh = MLA_HEADS_PER_STEP
    return pl.pallas_call(
        _mla_attn_kernel,
        grid=(batch, N_MLA_HEADS // nh),
        in_specs=[pl.BlockSpec((seq, nh * LANES), lambda b, p: (b, p)),
                  pl.BlockSpec((seq, nh * LANES), lambda b, p: (b, p)),
                  pl.BlockSpec((seq, nh * D_MLA_V), lambda b, p: (b, p))],
        out_specs=pl.BlockSpec((seq, nh * D_MLA_V), lambda b, p: (b, p)),
        out_shape=jax.ShapeDtypeStruct((t, N_MLA_HEADS * D_MLA_V), BF16),
        scratch_shapes=[pltpu.VMEM((nh, D_MLA_V + ONES_ROWS, seq), BF16)],
        compiler_params=_attn_params(),
        name="mla_attn",
    )(qm, km, vm)


def _diff_attn_call(lam_p, sg, qd, kd, vd, batch, seq, lam_init):
    t = qd.shape[0]
    nh = DIFF_HEADS_PER_STEP
    return pl.pallas_call(
        functools.partial(_diff_attn_kernel, lam_init=lam_init),
        grid=(batch, N_DIFF_HEADS // nh),
        in_specs=[pl.BlockSpec(lam_p.shape, lambda b, j: (0, 0)),
                  pl.BlockSpec(sg.shape, lambda b, j: (0, 0)),
                  pl.BlockSpec((seq, nh * LANES), lambda b, j: (b, j)),
                  pl.BlockSpec((seq, nh * LANES), lambda b, j: (b, j)),
                  pl.BlockSpec((seq, nh * LANES), lambda b, j: (b, j))],
        out_specs=pl.BlockSpec((seq, nh * LANES), lambda b, j: (b, j)),
        out_shape=jax.ShapeDtypeStruct((t, C_DV), BF16),
        scratch_shapes=[pltpu.VMEM((nh, 2 * D_DIFF_HEAD + ONES_ROWS, seq), BF16)],
        compiler_params=_attn_params(),
        name="diff_attn",
    )(lam_p, sg, qd, kd, vd)


def _post_kernel(x_ref, om_ref, od_ref, gate_ref, wbm_ref, wbd_ref, wo_ref, gpa_ref, gpm_ref,
                 wup_ref, wdn_ref, gpo_ref, o_ref):
    subs = [slice(r0, r0 + SUB_POST) for r0 in range(0, x_ref.shape[0], SUB_POST)]
    u = [(_dot(om_ref[s, :], wbm_ref[...]), _dot(od_ref[s, :], wbd_ref[...])) for s in subs]
    y = []
    for s, (u_mla, u_diff) in zip(subs, u):
        mixed = (gate_ref[s, :D_MODEL].astype(F32) * u_mla
                 + gate_ref[s, D_MODEL:].astype(F32) * u_diff).astype(BF16)
        y.append(_dot(mixed, wo_ref[...]))
    x1 = [x_ref[s, :] + _rms(yy, gpa_ref[...]) for s, yy in zip(subs, y)]
    h = [_rms(xx, gpm_ref[...]).astype(BF16) for xx in x1]
    m = [None] * len(subs)
    for c0 in range(0, D_FF, FF_CHUNK):
        ups = [jnp.maximum(_dot(hh, wup_ref[:, c0:c0 + FF_CHUNK]), 0.0) for hh in h]
        for i, up in enumerate(ups):
            part = _dot((up * up).astype(BF16), wdn_ref[c0:c0 + FF_CHUNK, :])
            m[i] = part if m[i] is None else m[i] + part
    for s, xx, mm in zip(subs, x1, m):
        o_ref[s, :] = xx + _rms(mm, gpo_ref[...])


def _post_call(x2, om, od, gate, wbm, wbd, wo, gpa, gpm, wup, wdn, gpo):
    t = x2.shape[0]
    tm = TM_POST
    row = lambda c: pl.BlockSpec((tm, c), lambda i: (i, 0))
    return pl.pallas_call(
        _post_kernel,
        grid=(t // tm,),
        in_specs=[row(D_MODEL), row(om.shape[1]), row(od.shape[1]), row(gate.shape[1]),
                  _const_spec(wbm.shape), _const_spec(wbd.shape), _const_spec(wo.shape),
                  _const_spec((1, D_MODEL)), _const_spec((1, D_MODEL)),
                  _const_spec(wup.shape), _const_spec(wdn.shape), _const_spec((1, D_MODEL))],
        out_specs=row(D_MODEL),
        out_shape=jax.ShapeDtypeStruct((t, D_MODEL), F32),
        compiler_params=pltpu.CompilerParams(
            dimension_semantics=("arbitrary",), vmem_limit_bytes=VMEM_LIMIT),
        name="post",
    )(x2, om, od, gate, wbm, wbd, wo, gpa, gpm, wup, wdn, gpo)


def _rot_half_cols(w):
    d = w.shape[-1]
    return jnp.concatenate([-w[..., d // 2:], w[..., :d // 2]], axis=-1)


def _rope_tables(seq):
    pos = np.arange(seq, dtype=np.float64)

    def cs(d):
        inv = 1.0 / (ROPE_THETA ** (np.arange(0, d, 2, dtype=np.float64) / d))
        ang = pos[:, None] * inv[None, :]
        return np.cos(ang), np.sin(ang)

    cd, sd = cs(D_DIFF_HEAD)
    cosd = np.tile(cd, (1, 4))
    sind = np.concatenate([-sd, sd, -sd, sd], axis=1)
    cm, sm = cs(D_MLA_ROPE)
    one = np.ones((seq, D_MLA_NOPE))
    zero_n = np.zeros((seq, D_MLA_NOPE))
    zero_p = np.zeros((seq, LANES - D_MLA_NOPE - D_MLA_ROPE))
    cosm = np.concatenate([one, cm, cm, zero_p], axis=1)
    sinm = np.concatenate([zero_n, sm, sm, zero_p], axis=1)
    return tuple(jnp.asarray(t, dtype=F32) for t in (cosd, sind, cosm, sinm))


def _layout_in_proj(w_in):
    wt = jnp.swapaxes(w_in, 0, 1).astype(BF16)
    d = wt.shape[1]
    w_kr = wt[R_KR:R_KR + D_MLA_ROPE]
    half = D_MLA_ROPE // 2
    rot = jnp.concatenate([-w_kr[half:], w_kr[:half]], axis=0)
    zn = jnp.zeros((D_MLA_NOPE, d), BF16)
    zp = jnp.zeros((LANES - D_MLA_NOPE - D_MLA_ROPE, d), BF16)
    return wt, jnp.concatenate([zn, w_kr, zp, zn, rot, zp], axis=0)


def _layout_q_b(w_q_b):
    r = w_q_b.shape[0]
    w = w_q_b.reshape(r, N_MLA_HEADS, D_MLA_NOPE + D_MLA_ROPE)
    nope, rp = w[..., :D_MLA_NOPE], w[..., D_MLA_NOPE:]
    zp = jnp.zeros((r, N_MLA_HEADS, LANES - D_MLA_NOPE - D_MLA_ROPE), w.dtype)
    zn = jnp.zeros((r, N_MLA_HEADS, D_MLA_NOPE), w.dtype)
    a = jnp.concatenate([nope, rp, zp], axis=-1).reshape(r, N_MLA_HEADS * LANES)
    b = jnp.concatenate([zn, _rot_half_cols(rp), zp], axis=-1).reshape(r, N_MLA_HEADS * LANES)
    return jnp.concatenate([a, b], axis=1).astype(BF16)


def _layout_kv_b(w_kv_b):
    r = w_kv_b.shape[0]
    w = w_kv_b.reshape(r, N_MLA_HEADS, D_MLA_NOPE + D_MLA_V)
    kn, v = w[..., :D_MLA_NOPE], w[..., D_MLA_NOPE:]
    zk = jnp.zeros((r, N_MLA_HEADS, LANES - D_MLA_NOPE), w.dtype)
    wk = jnp.concatenate([kn, zk], axis=-1).reshape(r, N_MLA_HEADS * LANES)
    wv = v.reshape(r, N_MLA_HEADS * D_MLA_V)
    return jnp.concatenate([wk, wv], axis=1).astype(BF16)


def kernel(x, pre_attn_g, w_in, q_norm_g, w_q_b, kv_norm_g, w_kv_b, lambda_q1, lambda_k1, lambda_q2, lambda_k2, subln_g, w_br_mla, w_br_diff, w_out, post_attn_g, pre_mlp_g, w_mlp_up, w_mlp_down, post_mlp_g):
    batch, seq, d_model = x.shape
    depth = w_in.shape[0]
    assert d_model == D_MODEL and seq % MLA_TQ == 0 and seq % DIFF_TQ == 0 and seq % TM_PROJ == 0
    assert MLA_TQ % MLA_TK == 0 and DIFF_TQ % DIFF_TK == 0 and MLA_TQ % DIAG_TK == 0 and DIFF_TQ % DIAG_TK == 0
    assert (batch * seq) % TM_POST == 0
    cosd, sind, cosm, sinm = _rope_tables(seq)
    x2 = x.reshape(batch * seq, d_model)
    row = lambda v: v.reshape(1, -1).astype(F32)
    for l in range(depth):
        lam_init = 0.8 - 0.6 * math.exp(-0.3 * l)
        qm, km, vm, qd, kd, vd, gate = _proj_call(
            x2, row(pre_attn_g[l]), *_layout_in_proj(w_in[l]), row(q_norm_g[l]),
            _layout_q_b(w_q_b[l]), row(kv_norm_g[l]), _layout_kv_b(w_kv_b[l]),
            cosd, sind, cosm, sinm, seq)
        om = _mla_attn_call(qm, km, vm, batch, seq)
        lam_p = jnp.stack([lambda_q1[l], lambda_k1[l], lambda_q2[l], lambda_k2[l]]).astype(F32)
        od = _diff_attn_call(lam_p, row(subln_g[l]), qd, kd, vd, batch, seq, lam_init)
        x2 = _post_call(
            x2, om, od, gate, w_br_mla[l].astype(BF16), w_br_diff[l].astype(BF16),
            w_out[l].astype(BF16), row(post_attn_g[l]), row(pre_mlp_g[l]),
            w_mlp_up[l].astype(BF16), w_mlp_down[l].astype(BF16), row(post_mlp_g[l]))
    return x2.reshape(batch, seq, d_model)
```

```python
import functools
import math

import numpy as np
import jax
import jax.numpy as jnp
from jax import lax
from jax.experimental import pallas as pl
from jax.experimental.pallas import tpu as pltpu

D_MODEL = 1024
N_MLA_HEADS = 8
D_MLA_NOPE = 64
D_MLA_ROPE = 32
D_MLA_V = 64
Q_LORA = 256
KV_LORA = 128
N_DIFF_HEADS = 8
D_DIFF_HEAD = 64
D_FF = 4 * D_MODEL
ROPE_THETA = 10000.0
EPS = 1e-6

LANES = 128
VMEM_LIMIT = 52 * 1024 * 1024

C_QA = Q_LORA
C_KVA = KV_LORA + D_MLA_ROPE
C_DQ = 2 * N_DIFF_HEADS * D_DIFF_HEAD
C_DK = C_DQ
C_DV = N_DIFF_HEADS * 2 * D_DIFF_HEAD
C_GATE = 2 * D_MODEL

R_QA = 0
R_CKV = R_QA + Q_LORA
R_KR = R_CKV + KV_LORA
R_DQ = R_KR + D_MLA_ROPE
R_DK = R_DQ + C_DQ
R_DV = R_DK + C_DK
R_GATE = R_DV + C_DV
D_IN_TOTAL = R_GATE + C_GATE

TM_PROJ = 512
SUB_PROJ = 256
TM_POST = 512
SUB_POST = 256
FF_CHUNK = 1024
MLA_TQ, MLA_TK, MLA_LOOKAHEAD = 512, 256, 5
DIFF_TQ, DIFF_TK, DIFF_LOOKAHEAD = 1024, 512, 2
MLA_HEADS_PER_STEP = 4
DIFF_HEADS_PER_STEP = 2
DIAG_TK = 256

LOG2E = math.log2(math.e)

F32 = jnp.float32
BF16 = jnp.bfloat16


def _rms(x, g):
    return x * lax.rsqrt(jnp.mean(x * x, axis=-1, keepdims=True) + EPS) * g


def _dot(a, b):
    return jnp.dot(a, b, preferred_element_type=F32)


def _dot_nt(a, b):
    return lax.dot_general(a, b, (((1,), (1,)), ((), ())), preferred_element_type=F32)


def _proj_kernel(x_ref, g_ref, wt_ref, krt_ref, qng_ref, wqb_ref, kvg_ref, wkvb_ref,
                 cosd_ref, sind_ref, cosm_ref, sinm_ref,
                 qm_ref, km_ref, vm_ref, qd_ref, kd_ref, vd_ref, gate_ref,
                 *, mla_scale, diff_scale):
    subs = [slice(r0, r0 + SUB_PROJ) for r0 in range(0, x_ref.shape[0], SUB_PROJ)]
    half = N_MLA_HEADS * LANES
    lane = lax.broadcasted_iota(jnp.int32, (1, LANES), 1)
    low_half = lane % D_DIFF_HEAD < D_DIFF_HEAD // 2
    mla_low_half = lane < D_MLA_NOPE + D_MLA_ROPE // 2

    def seg(hh, r0, n):
        return _dot_nt(hh, wt_ref[r0:r0 + n, :])

    def rope_mla(xg, cosm, sinm):
        up = pltpu.roll(xg, LANES - D_MLA_ROPE // 2, 1)
        down = pltpu.roll(xg, D_MLA_ROPE // 2, 1)
        return xg * cosm + jnp.where(mla_low_half, up, down) * sinm

    def rope_group(xg, cosd, sind):
        up = pltpu.roll(xg, LANES - D_DIFF_HEAD // 2, 1)
        down = pltpu.roll(xg, D_DIFF_HEAD // 2, 1)
        return xg * cosd + jnp.where(low_half, up, down) * sind

    h = [_rms(x_ref[s, :], g_ref[...]).astype(BF16) for s in subs]
    lat = [(seg(hh, R_QA, Q_LORA), seg(hh, R_CKV, KV_LORA),
            _dot_nt(hh, krt_ref[...]), seg(hh, R_GATE, C_GATE)) for hh in h]

    up_proj = []
    for s, hh, (qa, ckv, kr, gl) in zip(subs, h, lat):
        qn = _rms(qa, qng_ref[...]).astype(BF16)
        kn = _rms(ckv, kvg_ref[...]).astype(BF16)
        krp = rope_mla(kr, cosm_ref[s, :], sinm_ref[s, :])
        up_proj.append((_dot(qn, wqb_ref[...]), _dot(kn, wkvb_ref[...]), krp, seg(hh, R_DQ, C_DQ)))
        gate_ref[s, :] = (1.0 / (1.0 + jnp.exp(-gl))).astype(BF16)

    dks = []
    for s, hh, (qab, kvb, krp, dq) in zip(subs, h, up_proj):
        dks.append(seg(hh, R_DK, C_DK))
        cosm = cosm_ref[s, :]
        sinm = sinm_ref[s, :]
        for j in range(N_MLA_HEADS):
            a = rope_mla(qab[:, j * LANES:(j + 1) * LANES], cosm, sinm)
            qm_ref[s, j * LANES:(j + 1) * LANES] = (a * mla_scale).astype(BF16)
        for j in range(N_MLA_HEADS):
            km_ref[s, j * LANES:(j + 1) * LANES] = (kvb[:, j * LANES:(j + 1) * LANES] + krp).astype(BF16)
        vm_ref[s, :] = kvb[:, half:].astype(BF16)
        for j in range(N_DIFF_HEADS):
            r = rope_group(dq[:, j * LANES:(j + 1) * LANES], cosd_ref[s, :], sind_ref[s, :])
            qd_ref[s, j * LANES:(j + 1) * LANES] = (r * diff_scale).astype(BF16)

    for s, hh, dk in zip(subs, h, dks):
        dv = seg(hh, R_DV, C_DV)
        for j in range(N_DIFF_HEADS):
            kd_ref[s, j * LANES:(j + 1) * LANES] = rope_group(
                dk[:, j * LANES:(j + 1) * LANES], cosd_ref[s, :], sind_ref[s, :]).astype(BF16)
        vd_ref[s, :] = dv.astype(BF16)


def _const_spec(shape):
    return pl.BlockSpec(shape, lambda i: (0,) * len(shape), pipeline_mode=pl.Buffered(1))


def _proj_call(x2, g, wt, krt, qng, wqb, kvg, wkvb, cosd, sind, cosm, sinm, seq):
    t = x2.shape[0]
    tm = TM_PROJ
    tiles_per_seq = seq // tm
    row = lambda c: pl.BlockSpec((tm, c), lambda i: (i, 0))
    tab = pl.BlockSpec((tm, LANES), lambda i: (i % tiles_per_seq, 0))
    widths = (N_MLA_HEADS * LANES, N_MLA_HEADS * LANES, N_MLA_HEADS * D_MLA_V,
              C_DQ, C_DK, C_DV, C_GATE)
    kern = functools.partial(
        _proj_kernel,
        mla_scale=LOG2E / math.sqrt(D_MLA_NOPE + D_MLA_ROPE),
        diff_scale=LOG2E / math.sqrt(D_DIFF_HEAD))
    return pl.pallas_call(
        kern,
        grid=(t // tm,),
        in_specs=[row(D_MODEL), _const_spec((1, D_MODEL)), _const_spec(wt.shape), _const_spec(krt.shape),
                  _const_spec((1, Q_LORA)), _const_spec(wqb.shape),
                  _const_spec((1, KV_LORA)), _const_spec(wkvb.shape),
                  tab, tab, tab, tab],
        out_specs=[row(c) for c in widths],
        out_shape=[jax.ShapeDtypeStruct((t, c), BF16) for c in widths],
        compiler_params=pltpu.CompilerParams(
            dimension_semantics=("arbitrary",), vmem_limit_bytes=VMEM_LIMIT),
        name="proj",
    )(x2, g, wt, krt, qng, wqb, kvg, wkvb, cosd, sind, cosm, sinm)


ONES_ROWS = 16


def _transpose_bf16(x):
    return x.astype(F32).T.astype(BF16)


def _causal_attention_t(load_qts, seq, k_los, k_ref, vts, finish, tq, tk, lookahead):
    n = len(k_los)
    steps = []
    for qi in range(seq // tq):
        q0 = qi * tq
        blocks = [(kb * tk, tk, 0, False) for kb in range(q0 // tk)]
        blocks += [(q0 + d * DIAG_TK, DIAG_TK, d * DIAG_TK, True) for d in range(tq // DIAG_TK)]
        steps += [(q0,) + b + (j == 0, j == len(blocks) - 1) for j, b in enumerate(blocks)]
    units = [(step, i) for step in steps for i in range(n)]
    qts = {}

    def scores(step, i):
        q0, st, klen, c_lo, masked, first, _ = step
        if first and i == 0:
            qts[q0] = load_qts(q0, tq)
        s = _dot(k_ref[st:st + klen, k_los[i]:k_los[i] + LANES], qts[q0][i][:, c_lo:])
        if masked:
            r = lax.broadcasted_iota(jnp.int32, s.shape, 0)
            c = lax.broadcasted_iota(jnp.int32, s.shape, 1)
            s = jnp.where(r <= c, s, -jnp.inf)
        return s

    state = [None] * n
    pending = [scores(*u) for u in units[:lookahead]]
    for t, (step, i) in enumerate(units):
        q0, st, klen, c_lo, masked, first, last = step
        s = pending.pop(0)
        if t + lookahead < len(units):
            pending.append(scores(*units[t + lookahead]))
        vt = vts[i][:, st:st + klen]
        m_blk = jnp.max(s, axis=0, keepdims=True)
        if first:
            m_new = m_blk
            acc = _dot(vt, jnp.exp2(s - m_new).astype(BF16))
        else:
            m, acc = state[i]
            m_new = jnp.maximum(m[:, c_lo:], m_blk)
            p = jnp.exp2(s - m_new).astype(BF16)
            acc_new = jnp.exp2(m[:, c_lo:] - m_new) * acc[:, c_lo:] + _dot(vt, p)
            if c_lo:
                m_new = jnp.concatenate([m[:, :c_lo], m_new], axis=1)
                acc_new = jnp.concatenate([acc[:, :c_lo], acc_new], axis=1)
            acc = acc_new
        state[i] = (m_new, acc)
        if last and i == n - 1:
            finish(q0, [a for _, a in state])


def _mla_attn_kernel(q_ref, k_ref, v_ref, o_ref, vt_ref):
    dv = D_MLA_V
    seq = q_ref.shape[0]
    nh = MLA_HEADS_PER_STEP
    vt = v_ref[...].astype(F32).T
    ones = jnp.ones((ONES_ROWS, seq), BF16)
    for hh in range(nh):
        vt_ref[hh, :dv, :] = vt[hh * dv:(hh + 1) * dv].astype(BF16)
        vt_ref[hh, dv:, :] = ones

    def finish(q0, accs):
        ot = jnp.concatenate([a[:dv] / a[dv:dv + 1] for a in accs], axis=0)
        o_ref[q0:q0 + MLA_TQ, :] = ot.T.astype(BF16)

    def load_qts(q0, tq):
        return [_transpose_bf16(q_ref[q0:q0 + tq, hh * LANES:(hh + 1) * LANES]) for hh in range(nh)]

    _causal_attention_t(load_qts, seq, [hh * LANES for hh in range(nh)], k_ref,
                        [vt_ref.at[hh] for hh in range(nh)], finish, MLA_TQ, MLA_TK, MLA_LOOKAHEAD)


def _diff_attn_kernel(lam_ref, sg_ref, q_ref, k_ref, v_ref, o_ref, vt_ref, *, lam_init):
    dv = 2 * D_DIFF_HEAD
    seq = q_ref.shape[0]
    nh = DIFF_HEADS_PER_STEP
    for hh in range(nh):
        vt_ref[hh, :dv, :] = _transpose_bf16(v_ref[:, hh * LANES:(hh + 1) * LANES])
        vt_ref[hh, dv:, :] = jnp.ones((ONES_ROWS, seq), BF16)
    lp = lam_ref[...]
    lam = (jnp.exp(jnp.sum(lp[0:1] * lp[1:2], axis=-1, keepdims=True))
           - jnp.exp(jnp.sum(lp[2:3] * lp[3:4], axis=-1, keepdims=True)) + lam_init)
    gain = sg_ref[...] * (1.0 - lam_init)

    def finish(q0, accs):
        for hh in range(nh):
            a1, a2 = accs[2 * hh], accs[2 * hh + 1]
            o = a1[:dv] / a1[dv:dv + 1] - lam * (a2[:dv] / a2[dv:dv + 1])
            on = o * lax.rsqrt(jnp.mean(o * o, axis=0, keepdims=True) + EPS)
            o_ref[q0:q0 + DIFF_TQ, hh * LANES:(hh + 1) * LANES] = (on.T * gain).astype(BF16)

    def load_qts(q0, tq):
        out = []
        zero = jnp.zeros((D_DIFF_HEAD, tq), BF16)
        for hh in range(nh):
            qt = _transpose_bf16(q_ref[q0:q0 + tq, hh * LANES:(hh + 1) * LANES])
            out += [jnp.concatenate([qt[:D_DIFF_HEAD], zero], axis=0),
                    jnp.concatenate([zero, qt[D_DIFF_HEAD:]], axis=0)]
        return out

    k_los = [hh * LANES for hh in range(nh) for _ in range(2)]
    vts = [vt_ref.at[hh] for hh in range(nh) for _ in range(2)]
    _causal_attention_t(load_qts, seq, k_los, k_ref, vts, finish, DIFF_TQ, DIFF_TK, DIFF_LOOKAHEAD)


def _attn_params():
    return pltpu.CompilerParams(
        dimension_semantics=("arbitrary", "arbitrary"), vmem_limit_bytes=VMEM_LIMIT)


def _mla_attn_call(qm, km, vm, batch, seq):
    t = qm.shape[0]
    nh = MLA_HEADS_PER_STEP
    return pl.pallas_call(
        _mla_attn_kernel,
        grid=(batch, N_MLA_HEADS // nh),
        in_specs=[pl.BlockSpec((seq, nh * LANES), lambda b, p: (b, p)),
                  pl.BlockSpec((seq, nh * LANES), lambda b, p: (b, p)),
                  pl.BlockSpec((seq, nh * D_MLA_V), lambda b, p: (b, p))],
        out_specs=pl.BlockSpec((seq, nh * D_MLA_V), lambda b, p: (b, p)),
        out_shape=jax.ShapeDtypeStruct((t, N_MLA_HEADS * D_MLA_V), BF16),
        scratch_shapes=[pltpu.VMEM((nh, D_MLA_V + ONES_ROWS, seq), BF16)],
        compiler_params=_attn_params(),
        name="mla_attn",
    )(qm, km, vm)


def _diff_attn_call(lam_p, sg, qd, kd, vd, batch, seq, lam_init):
    t = qd.shape[0]
    nh = DIFF_HEADS_PER_STEP
    return pl.pallas_call(
        functools.partial(_diff_attn_kernel, lam_init=lam_init),
        grid=(batch, N_DIFF_HEADS // nh),
        in_specs=[pl.BlockSpec(lam_p.shape, lambda b, j: (0, 0)),
                  pl.BlockSpec(sg.shape, lambda b, j: (0, 0)),
                  pl.BlockSpec((seq, nh * LANES), lambda b, j: (b, j)),
                  pl.BlockSpec((seq, nh * LANES), lambda b, j: (b, j)),
                  pl.BlockSpec((seq, nh * LANES), lambda b, j: (b, j))],
        out_specs=pl.BlockSpec((seq, nh * LANES), lambda b, j: (b, j)),
        out_shape=jax.ShapeDtypeStruct((t, C_DV), BF16),
        scratch_shapes=[pltpu.VMEM((nh, 2 * D_DIFF_HEAD + ONES_ROWS, seq), BF16)],
        compiler_params=_attn_params(),
        name="diff_attn",
    )(lam_p, sg, qd, kd, vd)


def _post_kernel(x_ref, om_ref, od_ref, gate_ref, wbm_ref, wbd_ref, wo_ref, gpa_ref, gpm_ref,
                 wup_ref, wdn_ref, gpo_ref, o_ref):
    subs = [slice(r0, r0 + SUB_POST) for r0 in range(0, x_ref.shape[0], SUB_POST)]
    u = [(_dot(om_ref[s, :], wbm_ref[...]), _dot(od_ref[s, :], wbd_ref[...])) for s in subs]
    y = []
    for s, (u_mla, u_diff) in zip(subs, u):
        mixed = (gate_ref[s, :D_MODEL].astype(F32) * u_mla
                 + gate_ref[s, D_MODEL:].astype(F32) * u_diff).astype(BF16)
        y.append(_dot(mixed, wo_ref[...]))
    x1 = [x_ref[s, :] + _rms(yy, gpa_ref[...]) for s, yy in zip(subs, y)]
    h = [_rms(xx, gpm_ref[...]).astype(BF16) for xx in x1]
    m = [None] * len(subs)
    for c0 in range(0, D_FF, FF_CHUNK):
        ups = [jnp.maximum(_dot(hh, wup_ref[:, c0:c0 + FF_CHUNK]), 0.0) for hh in h]
        for i, up in enumerate(ups):
            part = _dot((up * up).astype(BF16), wdn_ref[c0:c0 + FF_CHUNK, :])
            m[i] = part if m[i] is None else m[i] + part
    for s, xx, mm in zip(subs, x1, m):
        o_ref[s, :] = xx + _rms(mm, gpo_ref[...])


def _post_call(x2, om, od, gate, wbm, wbd, wo, gpa, gpm, wup, wdn, gpo):
    t = x2.shape[0]
    tm = TM_POST
    row = lambda c: pl.BlockSpec((tm, c), lambda i: (i, 0))
    return pl.pallas_call(
        _post_kernel,
        grid=(t // tm,),
        in_specs=[row(D_MODEL), row(om.shape[1]), row(od.shape[1]), row(gate.shape[1]),
                  _const_spec(wbm.shape), _const_spec(wbd.shape), _const_spec(wo.shape),
                  _const_spec((1, D_MODEL)), _const_spec((1, D_MODEL)),
                  _const_spec(wup.shape), _const_spec(wdn.shape), _const_spec((1, D_MODEL))],
        out_specs=row(D_MODEL),
        out_shape=jax.ShapeDtypeStruct((t, D_MODEL), F32),
        compiler_params=pltpu.CompilerParams(
            dimension_semantics=("arbitrary",), vmem_limit_bytes=VMEM_LIMIT),
        name="post",
    )(x2, om, od, gate, wbm, wbd, wo, gpa, gpm, wup, wdn, gpo)


def _rope_tables(seq):
    pos = np.arange(seq, dtype=np.float64)

    def cs(d):
        inv = 1.0 / (ROPE_THETA ** (np.arange(0, d, 2, dtype=np.float64) / d))
        ang = pos[:, None] * inv[None, :]
        return np.cos(ang), np.sin(ang)

    cd, sd = cs(D_DIFF_HEAD)
    cosd = np.tile(cd, (1, 4))
    sind = np.concatenate([-sd, sd, -sd, sd], axis=1)
    cm, sm = cs(D_MLA_ROPE)
    one = np.ones((seq, D_MLA_NOPE))
    zero_n = np.zeros((seq, D_MLA_NOPE))
    zero_p = np.zeros((seq, LANES - D_MLA_NOPE - D_MLA_ROPE))
    cosm = np.concatenate([one, cm, cm, zero_p], axis=1)
    sinm = np.concatenate([zero_n, -sm, sm, zero_p], axis=1)
    return tuple(jnp.asarray(t, dtype=F32) for t in (cosd, sind, cosm, sinm))


def _layout_in_proj(w_in):
    wt = jnp.swapaxes(w_in, 0, 1).astype(BF16)
    d = wt.shape[1]
    w_kr = wt[R_KR:R_KR + D_MLA_ROPE]
    zn = jnp.zeros((D_MLA_NOPE, d), BF16)
    zp = jnp.zeros((LANES - D_MLA_NOPE - D_MLA_ROPE, d), BF16)
    return wt, jnp.concatenate([zn, w_kr, zp], axis=0)


def _layout_q_b(w_q_b):
    r = w_q_b.shape[0]
    w = w_q_b.reshape(r, N_MLA_HEADS, D_MLA_NOPE + D_MLA_ROPE)
    nope, rp = w[..., :D_MLA_NOPE], w[..., D_MLA_NOPE:]
    zp = jnp.zeros((r, N_MLA_HEADS, LANES - D_MLA_NOPE - D_MLA_ROPE), w.dtype)
    return jnp.concatenate([nope, rp, zp], axis=-1).reshape(r, N_MLA_HEADS * LANES).astype(BF16)


def _layout_kv_b(w_kv_b):
    r = w_kv_b.shape[0]
    w = w_kv_b.reshape(r, N_MLA_HEADS, D_MLA_NOPE + D_MLA_V)
    kn, v = w[..., :D_MLA_NOPE], w[..., D_MLA_NOPE:]
    zk = jnp.zeros((r, N_MLA_HEADS, LANES - D_MLA_NOPE), w.dtype)
    wk = jnp.concatenate([kn, zk], axis=-1).reshape(r, N_MLA_HEADS * LANES)
    wv = v.reshape(r, N_MLA_HEADS * D_MLA_V)
    return jnp.concatenate([wk, wv], axis=1).astype(BF16)


def kernel(x, pre_attn_g, w_in, q_norm_g, w_q_b, kv_norm_g, w_kv_b, lambda_q1, lambda_k1, lambda_q2, lambda_k2, subln_g, w_br_mla, w_br_diff, w_out, post_attn_g, pre_mlp_g, w_mlp_up, w_mlp_down, post_mlp_g):
    batch, seq, d_model = x.shape
    depth = w_in.shape[0]
    assert d_model == D_MODEL and seq % MLA_TQ == 0 and seq % DIFF_TQ == 0 and seq % TM_PROJ == 0
    assert MLA_TQ % MLA_TK == 0 and DIFF_TQ % DIFF_TK == 0 and MLA_TQ % DIAG_TK == 0 and DIFF_TQ % DIAG_TK == 0
    assert (batch * seq) % TM_POST == 0
    cosd, sind, cosm, sinm = _rope_tables(seq)
    x2 = x.reshape(batch * seq, d_model)
    row = lambda v: v.reshape(1, -1).astype(F32)
    for l in range(depth):
        lam_init = 0.8 - 0.6 * math.exp(-0.3 * l)
        qm, km, vm, qd, kd, vd, gate = _proj_call(
            x2, row(pre_attn_g[l]), *_layout_in_proj(w_in[l]), row(q_norm_g[l]),
            _layout_q_b(w_q_b[l]), row(kv_norm_g[l]), _layout_kv_b(w_kv_b[l]),
            cosd, sind, cosm, sinm, seq)
        om = _mla_attn_call(qm, km, vm, batch, seq)
        lam_p = jnp.stack([lambda_q1[l], lambda_k1[l], lambda_q2[l], lambda_k2[l]]).astype(F32)
        od = _diff_attn_call(lam_p, row(subln_g[l]), qd, kd, vd, batch, seq, lam_init)
        x2 = _post_call(
            x2, om, od, gate, w_br_mla[l].astype(BF16), w_br_diff[l].astype(BF16),
            w_out[l].astype(BF16), row(post_attn_g[l]), row(pre_mlp_g[l]),
            w_mlp_up[l].astype(BF16), w_mlp_down[l].astype(BF16), row(post_mlp_g[l]))
    return x2.reshape(batch, seq, d_model)
```

```python
import functools
import math

import numpy as np
import jax
import jax.numpy as jnp
from jax import lax
from jax.experimental import pallas as pl
from jax.experimental.pallas import tpu as pltpu

D_MODEL = 1024
N_MLA_HEADS = 8
D_MLA_NOPE = 64
D_MLA_ROPE = 32
D_MLA_V = 64
Q_LORA = 256
KV_LORA = 128
N_DIFF_HEADS = 8
D_DIFF_HEAD = 64
D_FF = 4 * D_MODEL
ROPE_THETA = 10000.0
EPS = 1e-6

LANES = 128
VMEM_LIMIT = 52 * 1024 * 1024

C_QA = Q_LORA
C_KVA = KV_LORA + D_MLA_ROPE
C_DQ = 2 * N_DIFF_HEADS * D_DIFF_HEAD
C_DK = C_DQ
C_DV = N_DIFF_HEADS * 2 * D_DIFF_HEAD
C_GATE = 2 * D_MODEL

R_QA = 0
R_CKV = R_QA + Q_LORA
R_KR = R_CKV + KV_LORA
R_DQ = R_KR + D_MLA_ROPE
R_DK = R_DQ + C_DQ
R_DV = R_DK + C_DK
R_GATE = R_DV + C_DV
D_IN_TOTAL = R_GATE + C_GATE

TM_PROJ = 512
SUB_PROJ = 256
TM_POST = 512
SUB_POST = 256
FF_CHUNK = 1024
MLA_TQ, MLA_TK, MLA_LOOKAHEAD = 512, 256, 5
DIFF_TQ, DIFF_TK, DIFF_LOOKAHEAD = 1024, 512, 2
MLA_HEADS_PER_STEP = 4
DIFF_HEADS_PER_STEP = 2
DIAG_TK = 256

LOG2E = math.log2(math.e)

F32 = jnp.float32
BF16 = jnp.bfloat16


def _rms(x, g):
    return x * lax.rsqrt(jnp.mean(x * x, axis=-1, keepdims=True) + EPS) * g


def _dot(a, b):
    return jnp.dot(a, b, preferred_element_type=F32)


def _dot_nt(a, b):
    return lax.dot_general(a, b, (((1,), (1,)), ((), ())), preferred_element_type=F32)


def _proj_kernel(x_ref, g_ref, wt_ref, qng_ref, wqb_ref, kvg_ref, wkvb_ref,
                 cosd_ref, sind_ref, cosm_ref, sinm_ref,
                 qm_ref, km_ref, vm_ref, qd_ref, kd_ref, vd_ref, gate_ref,
                 *, mla_scale, diff_scale):
    subs = [slice(r0, r0 + SUB_PROJ) for r0 in range(0, x_ref.shape[0], SUB_PROJ)]
    half = N_MLA_HEADS * LANES
    lane = lax.broadcasted_iota(jnp.int32, (1, LANES), 1)
    low_half = lane % D_DIFF_HEAD < D_DIFF_HEAD // 2
    mla_low_half = lane < D_MLA_NOPE + D_MLA_ROPE // 2

    def seg(hh, r0, n):
        return _dot_nt(hh, wt_ref[r0:r0 + n, :])

    def rope_mla(xg, cosm, sinm):
        up = pltpu.roll(xg, LANES - D_MLA_ROPE // 2, 1)
        down = pltpu.roll(xg, D_MLA_ROPE // 2, 1)
        return xg * cosm + jnp.where(mla_low_half, up, down) * sinm

    def rope_group(xg, cosd, sind):
        up = pltpu.roll(xg, LANES - D_DIFF_HEAD // 2, 1)
        down = pltpu.roll(xg, D_DIFF_HEAD // 2, 1)
        return xg * cosd + jnp.where(low_half, up, down) * sind

    h = [_rms(x_ref[s, :], g_ref[...]).astype(BF16) for s in subs]
    lat = [(seg(hh, R_QA, Q_LORA), seg(hh, R_CKV, 2 * LANES), seg(hh, R_GATE, C_GATE)) for hh in h]
    rope_lanes = (lane >= D_MLA_NOPE) & (lane < D_MLA_NOPE + D_MLA_ROPE)

    up_proj = []
    for s, hh, (qa, ckv_kr, gl) in zip(subs, h, lat):
        qn = _rms(qa, qng_ref[...]).astype(BF16)
        kn = _rms(ckv_kr[:, :KV_LORA], kvg_ref[...]).astype(BF16)
        kr = jnp.where(rope_lanes, pltpu.roll(ckv_kr[:, LANES:], D_MLA_NOPE, 1), 0.0)
        krp = rope_mla(kr, cosm_ref[s, :], sinm_ref[s, :])
        up_proj.append((_dot(qn, wqb_ref[...]), _dot(kn, wkvb_ref[...]), krp, seg(hh, R_DQ, C_DQ)))
        gate_ref[s, :] = (1.0 / (1.0 + jnp.exp(-gl))).astype(BF16)

    dks = []
    for s, hh, (qab, kvb, krp, dq) in zip(subs, h, up_proj):
        dks.append(seg(hh, R_DK, C_DK))
        cosm = cosm_ref[s, :]
        sinm = sinm_ref[s, :]
        for j in range(N_MLA_HEADS):
            a = rope_mla(qab[:, j * LANES:(j + 1) * LANES], cosm, sinm)
            qm_ref[s, j * LANES:(j + 1) * LANES] = (a * mla_scale).astype(BF16)
        for j in range(N_MLA_HEADS):
            km_ref[s, j * LANES:(j + 1) * LANES] = (kvb[:, j * LANES:(j + 1) * LANES] + krp).astype(BF16)
        vm_ref[s, :] = kvb[:, half:].astype(BF16)
        for j in range(N_DIFF_HEADS):
            r = rope_group(dq[:, j * LANES:(j + 1) * LANES], cosd_ref[s, :], sind_ref[s, :])
            qd_ref[s, j * LANES:(j + 1) * LANES] = (r * diff_scale).astype(BF16)

    for s, hh, dk in zip(subs, h, dks):
        dv = seg(hh, R_DV, C_DV)
        for j in range(N_DIFF_HEADS):
            kd_ref[s, j * LANES:(j + 1) * LANES] = rope_group(
                dk[:, j * LANES:(j + 1) * LANES], cosd_ref[s, :], sind_ref[s, :]).astype(BF16)
        vd_ref[s, :] = dv.astype(BF16)


def _const_spec(shape):
    return pl.BlockSpec(shape, lambda i: (0,) * len(shape), pipeline_mode=pl.Buffered(1))


def _proj_call(x2, g, wt, qng, wqb, kvg, wkvb, cosd, sind, cosm, sinm, seq):
    t = x2.shape[0]
    tm = TM_PROJ
    tiles_per_seq = seq // tm
    row = lambda c: pl.BlockSpec((tm, c), lambda i: (i, 0))
    tab = pl.BlockSpec((tm, LANES), lambda i: (i % tiles_per_seq, 0))
    widths = (N_MLA_HEADS * LANES, N_MLA_HEADS * LANES, N_MLA_HEADS * D_MLA_V,
              C_DQ, C_DK, C_DV, C_GATE)
    kern = functools.partial(
        _proj_kernel,
        mla_scale=LOG2E / math.sqrt(D_MLA_NOPE + D_MLA_ROPE),
        diff_scale=LOG2E / math.sqrt(D_DIFF_HEAD))
    return pl.pallas_call(
        kern,
        grid=(t // tm,),
        in_specs=[row(D_MODEL), _const_spec((1, D_MODEL)), _const_spec(wt.shape),
                  _const_spec((1, Q_LORA)), _const_spec(wqb.shape),
                  _const_spec((1, KV_LORA)), _const_spec(wkvb.shape),
                  tab, tab, tab, tab],
        out_specs=[row(c) for c in widths],
        out_shape=[jax.ShapeDtypeStruct((t, c), BF16) for c in widths],
        compiler_params=pltpu.CompilerParams(
            dimension_semantics=("arbitrary",), vmem_limit_bytes=VMEM_LIMIT),
        name="proj",
    )(x2, g, wt, qng, wqb, kvg, wkvb, cosd, sind, cosm, sinm)


ONES_ROWS = 16


def _transpose_bf16(x):
    return x.astype(F32).T.astype(BF16)


def _causal_attention_t(load_qts, seq, k_los, k_ref, vts, finish, tq, tk, lookahead):
    n = len(k_los)
    steps = []
    for qi in range(seq // tq):
        q0 = qi * tq
        blocks = [(kb * tk, tk, 0, False) for kb in range(q0 // tk)]
        blocks += [(q0 + d * DIAG_TK, DIAG_TK, d * DIAG_TK, True) for d in range(tq // DIAG_TK)]
        steps += [(q0,) + b + (j == 0, j == len(blocks) - 1) for j, b in enumerate(blocks)]
    units = [(step, i) for step in steps for i in range(n)]
    qts = {}

    def scores(step, i):
        q0, st, klen, c_lo, masked, first, _ = step
        if first and i == 0:
            qts[q0] = load_qts(q0, tq)
        s = _dot(k_ref[st:st + klen, k_los[i]:k_los[i] + LANES], qts[q0][i][:, c_lo:])
        if masked:
            r = lax.broadcasted_iota(jnp.int32, s.shape, 0)
            c = lax.broadcasted_iota(jnp.int32, s.shape, 1)
            s = jnp.where(r <= c, s, -jnp.inf)
        return s

    state = [None] * n
    pending = [scores(*u) for u in units[:lookahead]]
    for t, (step, i) in enumerate(units):
        q0, st, klen, c_lo, masked, first, last = step
        s = pending.pop(0)
        if t + lookahead < len(units):
            pending.append(scores(*units[t + lookahead]))
        vt = vts[i][:, st:st + klen]
        m_blk = jnp.max(s, axis=0, keepdims=True)
        if first:
            m_new = m_blk
            acc = _dot(vt, jnp.exp2(s - m_new).astype(BF16))
        else:
            m, acc = state[i]
            m_new = jnp.maximum(m[:, c_lo:], m_blk)
            p = jnp.exp2(s - m_new).astype(BF16)
            acc_new = jnp.exp2(m[:, c_lo:] - m_new) * acc[:, c_lo:] + _dot(vt, p)
            if c_lo:
                m_new = jnp.concatenate([m[:, :c_lo], m_new], axis=1)
                acc_new = jnp.concatenate([acc[:, :c_lo], acc_new], axis=1)
            acc = acc_new
        state[i] = (m_new, acc)
        if last and i == n - 1:
            finish(q0, [a for _, a in state])


def _mla_attn_kernel(q_ref, k_ref, v_ref, o_ref, vt_ref):
    dv = D_MLA_V
    seq = q_ref.shape[0]
    nh = MLA_HEADS_PER_STEP
    vt = v_ref[...].astype(F32).T
    ones = jnp.ones((ONES_ROWS, seq), BF16)
    for hh in range(nh):
        vt_ref[hh, :dv, :] = vt[hh * dv:(hh + 1) * dv].astype(BF16)
        vt_ref[hh, dv:, :] = ones

    def finish(q0, accs):
        ot = jnp.concatenate([a[:dv] / a[dv:dv + 1] for a in accs], axis=0)
        o_ref[q0:q0 + MLA_TQ, :] = ot.T.astype(BF16)

    def load_qts(q0, tq):
        return [_transpose_bf16(q_ref[q0:q0 + tq, hh * LANES:(hh + 1) * LANES]) for hh in range(nh)]

    _causal_attention_t(load_qts, seq, [hh * LANES for hh in range(nh)], k_ref,
                        [vt_ref.at[hh] for hh in range(nh)], finish, MLA_TQ, MLA_TK, MLA_LOOKAHEAD)


def _diff_attn_kernel(lam_ref, sg_ref, q_ref, k_ref, v_ref, o_ref, vt_ref, *, lam_init):
    dv = 2 * D_DIFF_HEAD
    seq = q_ref.shape[0]
    nh = DIFF_HEADS_PER_STEP
    for hh in range(nh):
        vt_ref[hh, :dv, :] = _transpose_bf16(v_ref[:, hh * LANES:(hh + 1) * LANES])
        vt_ref[hh, dv:, :] = jnp.ones((ONES_ROWS, seq), BF16)
    lp = lam_ref[...]
    lam = (jnp.exp(jnp.sum(lp[0:1] * lp[1:2], axis=-1, keepdims=True))
           - jnp.exp(jnp.sum(lp[2:3] * lp[3:4], axis=-1, keepdims=True)) + lam_init)
    gain = sg_ref[...] * (1.0 - lam_init)

    def finish(q0, accs):
        for hh in range(nh):
            a1, a2 = accs[2 * hh], accs[2 * hh + 1]
            o = a1[:dv] / a1[dv:dv + 1] - lam * (a2[:dv] / a2[dv:dv + 1])
            on = o * lax.rsqrt(jnp.mean(o * o, axis=0, keepdims=True) + EPS)
            o_ref[q0:q0 + DIFF_TQ, hh * LANES:(hh + 1) * LANES] = (on.T * gain).astype(BF16)

    def load_qts(q0, tq):
        out = []
        zero = jnp.zeros((D_DIFF_HEAD, tq), BF16)
        for hh in range(nh):
            qt = _transpose_bf16(q_ref[q0:q0 + tq, hh * LANES:(hh + 1) * LANES])
            out += [jnp.concatenate([qt[:D_DIFF_HEAD], zero], axis=0),
                    jnp.concatenate([zero, qt[D_DIFF_HEAD:]], axis=0)]
        return out

    k_los = [hh * LANES for hh in range(nh) for _ in range(2)]
    vts = [vt_ref.at[hh] for hh in range(nh) for _ in range(2)]
    _causal_attention_t(load_qts, seq, k_los, k_ref, vts, finish, DIFF_TQ, DIFF_TK, DIFF_LOOKAHEAD)


def _attn_params():
    return pltpu.CompilerParams(
        dimension_semantics=("arbitrary", "arbitrary"), vmem_limit_bytes=VMEM_LIMIT)


def _mla_attn_call(qm, km, vm, batch, seq):
    t = qm.shape[0]
    nh = MLA_HEADS_PER_STEP
    return pl.pallas_call(
        _mla_attn_kernel,
        grid=(batch, N_MLA_HEADS // nh),
        in_specs=[pl.BlockSpec((seq, nh * LANES), lambda b, p: (b, p)),
                  pl.BlockSpec((seq, nh * LANES), lambda b, p: (b, p)),
                  pl.BlockSpec((seq, nh * D_MLA_V), lambda b, p: (b, p))],
        out_specs=pl.BlockSpec((seq, nh * D_MLA_V), lambda b, p: (b, p)),
        out_shape=jax.ShapeDtypeStruct((t, N_MLA_HEADS * D_MLA_V), BF16),
        scratch_shapes=[pltpu.VMEM((nh, D_MLA_V + ONES_ROWS, seq), BF16)],
        compiler_params=_attn_params(),
        name="mla_attn",
    )(qm, km, vm)


def _diff_attn_call(lam_p, sg, qd, kd, vd, batch, seq, lam_init):
    t = qd.shape[0]
    nh = DIFF_HEADS_PER_STEP
    return pl.pallas_call(
        functools.partial(_diff_attn_kernel, lam_init=lam_init),
        grid=(batch, N_DIFF_HEADS // nh),
        in_specs=[pl.BlockSpec(lam_p.shape, lambda b, j: (0, 0)),
                  pl.BlockSpec(sg.shape, lambda b, j: (0, 0)),
                  pl.BlockSpec((seq, nh * LANES), lambda b, j: (b, j)),
                  pl.BlockSpec((seq, nh * LANES), lambda b, j: (b, j)),
                  pl.BlockSpec((seq, nh * LANES), lambda b, j: (b, j))],
        out_specs=pl.BlockSpec((seq, nh * LANES), lambda b, j: (b, j)),
        out_shape=jax.ShapeDtypeStruct((t, C_DV), BF16),
        scratch_shapes=[pltpu.VMEM((nh, 2 * D_DIFF_HEAD + ONES_ROWS, seq), BF16)],
        compiler_params=_attn_params(),
        name="diff_attn",
    )(lam_p, sg, qd, kd, vd)


def _post_kernel(x_ref, om_ref, od_ref, gate_ref, wbm_ref, wbd_ref, wo_ref, gpa_ref, gpm_ref,
                 wup_ref, wdn_ref, gpo_ref, o_ref):
    subs = [slice(r0, r0 + SUB_POST) for r0 in range(0, x_ref.shape[0], SUB_POST)]
    u = [(_dot(om_ref[s, :], wbm_ref[...]), _dot(od_ref[s, :], wbd_ref[...])) for s in subs]
    y = []
    for s, (u_mla, u_diff) in zip(subs, u):
        mixed = (gate_ref[s, :D_MODEL].astype(F32) * u_mla
                 + gate_ref[s, D_MODEL:].astype(F32) * u_diff).astype(BF16)
        y.append(_dot(mixed, wo_ref[...]))
    x1 = [x_ref[s, :] + _rms(yy, gpa_ref[...]) for s, yy in zip(subs, y)]
    h = [_rms(xx, gpm_ref[...]).astype(BF16) for xx in x1]
    m = [None] * len(subs)
    for c0 in range(0, D_FF, FF_CHUNK):
        ups = [jnp.maximum(_dot(hh, wup_ref[:, c0:c0 + FF_CHUNK]), 0.0) for hh in h]
        for i, up in enumerate(ups):
            part = _dot((up * up).astype(BF16), wdn_ref[c0:c0 + FF_CHUNK, :])
            m[i] = part if m[i] is None else m[i] + part
    for s, xx, mm in zip(subs, x1, m):
        o_ref[s, :] = xx + _rms(mm, gpo_ref[...])


def _post_call(x2, om, od, gate, wbm, wbd, wo, gpa, gpm, wup, wdn, gpo):
    t = x2.shape[0]
    tm = TM_POST
    row = lambda c: pl.BlockSpec((tm, c), lambda i: (i, 0))
    return pl.pallas_call(
        _post_kernel,
        grid=(t // tm,),
        in_specs=[row(D_MODEL), row(om.shape[1]), row(od.shape[1]), row(gate.shape[1]),
                  _const_spec(wbm.shape), _const_spec(wbd.shape), _const_spec(wo.shape),
                  _const_spec((1, D_MODEL)), _const_spec((1, D_MODEL)),
                  _const_spec(wup.shape), _const_spec(wdn.shape), _const_spec((1, D_MODEL))],
        out_specs=row(D_MODEL),
        out_shape=jax.ShapeDtypeStruct((t, D_MODEL), F32),
        compiler_params=pltpu.CompilerParams(
            dimension_semantics=("arbitrary",), vmem_limit_bytes=VMEM_LIMIT),
        name="post",
    )(x2, om, od, gate, wbm, wbd, wo, gpa, gpm, wup, wdn, gpo)


def _rope_tables(seq):
    pos = np.arange(seq, dtype=np.float64)

    def cs(d):
        inv = 1.0 / (ROPE_THETA ** (np.arange(0, d, 2, dtype=np.float64) / d))
        ang = pos[:, None] * inv[None, :]
        return np.cos(ang), np.sin(ang)

    cd, sd = cs(D_DIFF_HEAD)
    cosd = np.tile(cd, (1, 4))
    sind = np.concatenate([-sd, sd, -sd, sd], axis=1)
    cm, sm = cs(D_MLA_ROPE)
    one = np.ones((seq, D_MLA_NOPE))
    zero_n = np.zeros((seq, D_MLA_NOPE))
    zero_p = np.zeros((seq, LANES - D_MLA_NOPE - D_MLA_ROPE))
    cosm = np.concatenate([one, cm, cm, zero_p], axis=1)
    sinm = np.concatenate([zero_n, -sm, sm, zero_p], axis=1)
    return tuple(jnp.asarray(t, dtype=F32) for t in (cosd, sind, cosm, sinm))


def _layout_in_proj(w_in):
    return jnp.swapaxes(w_in, 0, 1).astype(BF16)


def _layout_q_b(w_q_b):
    r = w_q_b.shape[0]
    w = w_q_b.reshape(r, N_MLA_HEADS, D_MLA_NOPE + D_MLA_ROPE)
    nope, rp = w[..., :D_MLA_NOPE], w[..., D_MLA_NOPE:]
    zp = jnp.zeros((r, N_MLA_HEADS, LANES - D_MLA_NOPE - D_MLA_ROPE), w.dtype)
    return jnp.concatenate([nope, rp, zp], axis=-1).reshape(r, N_MLA_HEADS * LANES).astype(BF16)


def _layout_kv_b(w_kv_b):
    r = w_kv_b.shape[0]
    w = w_kv_b.reshape(r, N_MLA_HEADS, D_MLA_NOPE + D_MLA_V)
    kn, v = w[..., :D_MLA_NOPE], w[..., D_MLA_NOPE:]
    zk = jnp.zeros((r, N_MLA_HEADS, LANES - D_MLA_NOPE), w.dtype)
    wk = jnp.concatenate([kn, zk], axis=-1).reshape(r, N_MLA_HEADS * LANES)
    wv = v.reshape(r, N_MLA_HEADS * D_MLA_V)
    return jnp.concatenate([wk, wv], axis=1).astype(BF16)


def kernel(x, pre_attn_g, w_in, q_norm_g, w_q_b, kv_norm_g, w_kv_b, lambda_q1, lambda_k1, lambda_q2, lambda_k2, subln_g, w_br_mla, w_br_diff, w_out, post_attn_g, pre_mlp_g, w_mlp_up, w_mlp_down, post_mlp_g):
    batch, seq, d_model = x.shape
    depth = w_in.shape[0]
    assert d_model == D_MODEL and seq % MLA_TQ == 0 and seq % DIFF_TQ == 0 and seq % TM_PROJ == 0
    assert MLA_TQ % MLA_TK == 0 and DIFF_TQ % DIFF_TK == 0 and MLA_TQ % DIAG_TK == 0 and DIFF_TQ % DIAG_TK == 0
    assert (batch * seq) % TM_POST == 0
    cosd, sind, cosm, sinm = _rope_tables(seq)
    x2 = x.reshape(batch * seq, d_model)
    row = lambda v: v.reshape(1, -1).astype(F32)
    for l in range(depth):
        lam_init = 0.8 - 0.6 * math.exp(-0.3 * l)
        qm, km, vm, qd, kd, vd, gate = _proj_call(
            x2, row(pre_attn_g[l]), _layout_in_proj(w_in[l]), row(q_norm_g[l]),
            _layout_q_b(w_q_b[l]), row(kv_norm_g[l]), _layout_kv_b(w_kv_b[l]),
            cosd, sind, cosm, sinm, seq)
        om = _mla_attn_call(qm, km, vm, batch, seq)
        lam_p = jnp.stack([lambda_q1[l], lambda_k1[l], lambda_q2[l], lambda_k2[l]]).astype(F32)
        od = _diff_attn_call(lam_p, row(subln_g[l]), qd, kd, vd, batch, seq, lam_init)
        x2 = _post_call(
            x2, om, od, gate, w_br_mla[l].astype(BF16), w_br_diff[l].astype(BF16),
            w_out[l].astype(BF16), row(post_attn_g[l]), row(pre_mlp_g[l]),
            w_mlp_up[l].astype(BF16), w_mlp_down[l].astype(BF16), row(post_mlp_g[l]))
    return x2.reshape(batch, seq, d_model)
```

```python
import functools
import math

import numpy as np
import jax
import jax.numpy as jnp
from jax import lax
from jax.experimental import pallas as pl
from jax.experimental.pallas import tpu as pltpu

D_MODEL = 1024
N_MLA_HEADS = 8
D_MLA_NOPE = 64
D_MLA_ROPE = 32
D_MLA_V = 64
Q_LORA = 256
KV_LORA = 128
N_DIFF_HEADS = 8
D_DIFF_HEAD = 64
D_FF = 4 * D_MODEL
ROPE_THETA = 10000.0
EPS = 1e-6

LANES = 128
VMEM_LIMIT = 52 * 1024 * 1024

C_DQ = 2 * N_DIFF_HEADS * D_DIFF_HEAD
C_DK = C_DQ
C_DV = N_DIFF_HEADS * 2 * D_DIFF_HEAD
C_GATE = 2 * D_MODEL

R_QA = 0
R_CKV = R_QA + Q_LORA
R_KR = R_CKV + KV_LORA
R_DQ = R_KR + D_MLA_ROPE
R_DK = R_DQ + C_DQ
R_DV = R_DK + C_DK
R_GATE = R_DV + C_DV
D_IN_TOTAL = R_GATE + C_GATE

TM_PROJ = 512
SUB_PROJ = 256
TM_POST = 512
SUB_POST = 256
FF_CHUNK = 1024
MLA_TQ, MLA_TK, MLA_LOOKAHEAD = 512, 256, 5
DIFF_TQ, DIFF_TK, DIFF_LOOKAHEAD = 1024, 512, 2
MLA_HEADS_PER_STEP = 4
DIFF_HEADS_PER_STEP = 2
DIAG_TK = 256

LOG2E = math.log2(math.e)

F32 = jnp.float32
BF16 = jnp.bfloat16


def _rms(x, g):
    return x * lax.rsqrt(jnp.mean(x * x, axis=-1, keepdims=True) + EPS) * g


def _dot(a, b):
    return jnp.dot(a, b, preferred_element_type=F32)


def _dot_nt(a, b):
    return lax.dot_general(a, b, (((1,), (1,)), ((), ())), preferred_element_type=F32)


def _proj_kernel(x_ref, g_ref, wt_ref, qng_ref, wqb_ref, kvg_ref, wkvb_ref,
                 cosd_ref, sind_ref, cosm_ref, sinm_ref,
                 qm_ref, km_ref, vm_ref, qd_ref, kd_ref, vd_ref, gate_ref,
                 *, mla_scale, diff_scale):
    subs = [slice(r0, r0 + SUB_PROJ) for r0 in range(0, x_ref.shape[0], SUB_PROJ)]
    half = N_MLA_HEADS * LANES
    lane = lax.broadcasted_iota(jnp.int32, (1, LANES), 1)
    low_half = lane % D_DIFF_HEAD < D_DIFF_HEAD // 2
    mla_low_half = lane < D_MLA_NOPE + D_MLA_ROPE // 2

    def seg(hh, r0, n):
        return _dot_nt(hh, wt_ref[r0:r0 + n, :])

    def rope_mla(xg, cosm, sinm):
        up = pltpu.roll(xg, LANES - D_MLA_ROPE // 2, 1)
        down = pltpu.roll(xg, D_MLA_ROPE // 2, 1)
        return xg * cosm + jnp.where(mla_low_half, up, down) * sinm

    def rope_group(xg, cosd, sind):
        up = pltpu.roll(xg, LANES - D_DIFF_HEAD // 2, 1)
        down = pltpu.roll(xg, D_DIFF_HEAD // 2, 1)
        return xg * cosd + jnp.where(low_half, up, down) * sind

    h = [_rms(x_ref[s, :], g_ref[...]).astype(BF16) for s in subs]
    lat = [(seg(hh, R_QA, Q_LORA), seg(hh, R_CKV, 2 * LANES), seg(hh, R_GATE, C_GATE)) for hh in h]
    rope_lanes = (lane >= D_MLA_NOPE) & (lane < D_MLA_NOPE + D_MLA_ROPE)

    up_proj = []
    for s, hh, (qa, ckv_kr, gl) in zip(subs, h, lat):
        qn = _rms(qa, qng_ref[...]).astype(BF16)
        kn = _rms(ckv_kr[:, :KV_LORA], kvg_ref[...]).astype(BF16)
        kr = jnp.where(rope_lanes, pltpu.roll(ckv_kr[:, LANES:], D_MLA_NOPE, 1), 0.0)
        krp = rope_mla(kr, cosm_ref[s, :], sinm_ref[s, :])
        up_proj.append((_dot(qn, wqb_ref[...]), _dot(kn, wkvb_ref[...]), krp, seg(hh, R_DQ, C_DQ)))
        gate_ref[s, :] = (1.0 / (1.0 + jnp.exp(-gl))).astype(BF16)

    dks = []
    for s, hh, (qab, kvb, krp, dq) in zip(subs, h, up_proj):
        dks.append(seg(hh, R_DK, C_DK))
        cosm = cosm_ref[s, :]
        sinm = sinm_ref[s, :]
        for j in range(N_MLA_HEADS):
            a = rope_mla(qab[:, j * LANES:(j + 1) * LANES], cosm, sinm)
            qm_ref[s, j * LANES:(j + 1) * LANES] = (a * mla_scale).astype(BF16)
        for j in range(N_MLA_HEADS):
            km_ref[s, j * LANES:(j + 1) * LANES] = (kvb[:, j * LANES:(j + 1) * LANES] + krp).astype(BF16)
        vm_ref[s, :] = kvb[:, half:].astype(BF16)
        for j in range(N_DIFF_HEADS):
            r = rope_group(dq[:, j * LANES:(j + 1) * LANES], cosd_ref[s, :], sind_ref[s, :])
            qd_ref[s, j * LANES:(j + 1) * LANES] = (r * diff_scale).astype(BF16)

    for s, hh, dk in zip(subs, h, dks):
        dv = seg(hh, R_DV, C_DV)
        for j in range(N_DIFF_HEADS):
            kd_ref[s, j * LANES:(j + 1) * LANES] = rope_group(
                dk[:, j * LANES:(j + 1) * LANES], cosd_ref[s, :], sind_ref[s, :]).astype(BF16)
        vd_ref[s, :] = dv.astype(BF16)


def _const_spec(shape):
    return pl.BlockSpec(shape, lambda i: (0,) * len(shape), pipeline_mode=pl.Buffered(1))


def _proj_call(x2, g, wt, qng, wqb, kvg, wkvb, cosd, sind, cosm, sinm, seq):
    t = x2.shape[0]
    tm = TM_PROJ
    tiles_per_seq = seq // tm
    row = lambda c: pl.BlockSpec((tm, c), lambda i: (i, 0))
    tab = pl.BlockSpec((tm, LANES), lambda i: (i % tiles_per_seq, 0))
    widths = (N_MLA_HEADS * LANES, N_MLA_HEADS * LANES, N_MLA_HEADS * D_MLA_V,
              C_DQ, C_DK, C_DV, C_GATE)
    kern = functools.partial(
        _proj_kernel,
        mla_scale=LOG2E / math.sqrt(D_MLA_NOPE + D_MLA_ROPE),
        diff_scale=LOG2E / math.sqrt(D_DIFF_HEAD))
    return pl.pallas_call(
        kern,
        grid=(t // tm,),
        in_specs=[row(D_MODEL), _const_spec((1, D_MODEL)), _const_spec(wt.shape),
                  _const_spec((1, Q_LORA)), _const_spec(wqb.shape),
                  _const_spec((1, KV_LORA)), _const_spec(wkvb.shape),
                  tab, tab, tab, tab],
        out_specs=[row(c) for c in widths],
        out_shape=[jax.ShapeDtypeStruct((t, c), BF16) for c in widths],
        compiler_params=pltpu.CompilerParams(
            dimension_semantics=("arbitrary",), vmem_limit_bytes=VMEM_LIMIT),
        name="proj",
    )(x2, g, wt, qng, wqb, kvg, wkvb, cosd, sind, cosm, sinm)


ONES_ROWS = 16


def _transpose_bf16(x):
    return x.astype(F32).T.astype(BF16)


def _causal_attention_t(load_qts, seq, k_los, k_ref, vts, finish, tq, tk, lookahead):
    n = len(k_los)
    steps = []
    for qi in range(seq // tq):
        q0 = qi * tq
        blocks = [(kb * tk, tk, 0, False) for kb in range(q0 // tk)]
        blocks += [(q0 + d * DIAG_TK, DIAG_TK, d * DIAG_TK, True) for d in range(tq // DIAG_TK)]
        steps += [(q0,) + b + (j == 0, j == len(blocks) - 1) for j, b in enumerate(blocks)]
    units = [(step, i) for step in steps for i in range(n)]
    qts = {}

    def scores(step, i):
        q0, st, klen, c_lo, masked, first, _ = step
        if first and i == 0:
            qts[q0] = load_qts(q0, tq)
        s = _dot(k_ref[st:st + klen, k_los[i]:k_los[i] + LANES], qts[q0][i][:, c_lo:])
        if masked:
            r = lax.broadcasted_iota(jnp.int32, s.shape, 0)
            c = lax.broadcasted_iota(jnp.int32, s.shape, 1)
            s = jnp.where(r <= c, s, -jnp.inf)
        return s

    state = [None] * n
    pending = [scores(*u) for u in units[:lookahead]]
    for t, (step, i) in enumerate(units):
        q0, st, klen, c_lo, masked, first, last = step
        s = pending.pop(0)
        if t + lookahead < len(units):
            pending.append(scores(*units[t + lookahead]))
        vt = vts[i][:, st:st + klen]
        m_blk = jnp.max(s, axis=0, keepdims=True)
        if first:
            m_new = m_blk
            acc = _dot(vt, jnp.exp2(s - m_new).astype(BF16))
        else:
            m, acc = state[i]
            m_new = jnp.maximum(m[:, c_lo:], m_blk)
            p = jnp.exp2(s - m_new).astype(BF16)
            acc_new = jnp.exp2(m[:, c_lo:] - m_new) * acc[:, c_lo:] + _dot(vt, p)
            if c_lo:
                m_new = jnp.concatenate([m[:, :c_lo], m_new], axis=1)
                acc_new = jnp.concatenate([acc[:, :c_lo], acc_new], axis=1)
            acc = acc_new
        state[i] = (m_new, acc)
        if last and i == n - 1:
            finish(q0, [a for _, a in state])


def _mla_attn_kernel(q_ref, k_ref, v_ref, o_ref, vt_ref):
    dv = D_MLA_V
    seq = q_ref.shape[0]
    nh = MLA_HEADS_PER_STEP
    vt = v_ref[...].astype(F32).T
    ones = jnp.ones((ONES_ROWS, seq), BF16)
    for hh in range(nh):
        vt_ref[hh, :dv, :] = vt[hh * dv:(hh + 1) * dv].astype(BF16)
        vt_ref[hh, dv:, :] = ones

    def finish(q0, accs):
        ot = jnp.concatenate([a[:dv] / a[dv:dv + 1] for a in accs], axis=0)
        o_ref[q0:q0 + MLA_TQ, :] = ot.T.astype(BF16)

    def load_qts(q0, tq):
        return [_transpose_bf16(q_ref[q0:q0 + tq, hh * LANES:(hh + 1) * LANES]) for hh in range(nh)]

    _causal_attention_t(load_qts, seq, [hh * LANES for hh in range(nh)], k_ref,
                        [vt_ref.at[hh] for hh in range(nh)], finish, MLA_TQ, MLA_TK, MLA_LOOKAHEAD)


def _diff_attn_kernel(lam_ref, sg_ref, q_ref, k_ref, v_ref, o_ref, vt_ref, *, lam_init):
    dv = 2 * D_DIFF_HEAD
    seq = q_ref.shape[0]
    nh = DIFF_HEADS_PER_STEP
    for hh in range(nh):
        vt_ref[hh, :dv, :] = _transpose_bf16(v_ref[:, hh * LANES:(hh + 1) * LANES])
        vt_ref[hh, dv:, :] = jnp.ones((ONES_ROWS, seq), BF16)
    lp = lam_ref[...]
    lam = (jnp.exp(jnp.sum(lp[0:1] * lp[1:2], axis=-1, keepdims=True))
           - jnp.exp(jnp.sum(lp[2:3] * lp[3:4], axis=-1, keepdims=True)) + lam_init)
    gain = sg_ref[...] * (1.0 - lam_init)

    def finish(q0, accs):
        for hh in range(nh):
            a1, a2 = accs[2 * hh], accs[2 * hh + 1]
            o = a1[:dv] / a1[dv:dv + 1] - lam * (a2[:dv] / a2[dv:dv + 1])
            on = o * lax.rsqrt(jnp.mean(o * o, axis=0, keepdims=True) + EPS)
            o_ref[q0:q0 + DIFF_TQ, hh * LANES:(hh + 1) * LANES] = (on.T * gain).astype(BF16)

    def load_qts(q0, tq):
        out = []
        zero = jnp.zeros((D_DIFF_HEAD, tq), BF16)
        for hh in range(nh):
            qt = _transpose_bf16(q_ref[q0:q0 + tq, hh * LANES:(hh + 1) * LANES])
            out += [jnp.concatenate([qt[:D_DIFF_HEAD], zero], axis=0),
                    jnp.concatenate([zero, qt[D_DIFF_HEAD:]], axis=0)]
        return out

    k_los = [hh * LANES for hh in range(nh) for _ in range(2)]
    vts = [vt_ref.at[hh] for hh in range(nh) for _ in range(2)]
    _causal_attention_t(load_qts, seq, k_los, k_ref, vts, finish, DIFF_TQ, DIFF_TK, DIFF_LOOKAHEAD)


def _attn_params():
    return pltpu.CompilerParams(
        dimension_semantics=("arbitrary", "arbitrary"), vmem_limit_bytes=VMEM_LIMIT)


def _mla_attn_call(qm, km, vm, batch, seq):
    t = qm.shape[0]
    nh = MLA_HEADS_PER_STEP
    return pl.pallas_call(
        _mla_attn_kernel,
        grid=(batch, N_MLA_HEADS // nh),
        in_specs=[pl.BlockSpec((seq, nh * LANES), lambda b, p: (b, p)),
                  pl.BlockSpec((seq, nh * LANES), lambda b, p: (b, p)),
                  pl.BlockSpec((seq, nh * D_MLA_V), lambda b, p: (b, p))],
        out_specs=pl.BlockSpec((seq, nh * D_MLA_V), lambda b, p: (b, p)),
        out_shape=jax.ShapeDtypeStruct((t, N_MLA_HEADS * D_MLA_V), BF16),
        scratch_shapes=[pltpu.VMEM((nh, D_MLA_V + ONES_ROWS, seq), BF16)],
        compiler_params=_attn_params(),
        name="mla_attn",
    )(qm, km, vm)


def _diff_attn_call(lam_p, sg, qd, kd, vd, batch, seq, lam_init):
    t = qd.shape[0]
    nh = DIFF_HEADS_PER_STEP
    return pl.pallas_call(
        functools.partial(_diff_attn_kernel, lam_init=lam_init),
        grid=(batch, N_DIFF_HEADS // nh),
        in_specs=[pl.BlockSpec(lam_p.shape, lambda b, j: (0, 0)),
                  pl.BlockSpec(sg.shape, lambda b, j: (0, 0)),
                  pl.BlockSpec((seq, nh * LANES), lambda b, j: (b, j)),
                  pl.BlockSpec((seq, nh * LANES), lambda b, j: (b, j)),
                  pl.BlockSpec((seq, nh * LANES), lambda b, j: (b, j))],
        out_specs=pl.BlockSpec((seq, nh * LANES), lambda b, j: (b, j)),
        out_shape=jax.ShapeDtypeStruct((t, C_DV), BF16),
        scratch_shapes=[pltpu.VMEM((nh, 2 * D_DIFF_HEAD + ONES_ROWS, seq), BF16)],
        compiler_params=_attn_params(),
        name="diff_attn",
    )(lam_p, sg, qd, kd, vd)


def _post_kernel(x_ref, om_ref, od_ref, gate_ref, wbm_ref, wbd_ref, wo_ref, gpa_ref, gpm_ref,
                 wup_ref, wdn_ref, gpo_ref, o_ref):
    subs = [slice(r0, r0 + SUB_POST) for r0 in range(0, x_ref.shape[0], SUB_POST)]
    u = [(_dot(om_ref[s, :], wbm_ref[...]), _dot(od_ref[s, :], wbd_ref[...])) for s in subs]
    y = []
    for s, (u_mla, u_diff) in zip(subs, u):
        mixed = (gate_ref[s, :D_MODEL].astype(F32) * u_mla
                 + gate_ref[s, D_MODEL:].astype(F32) * u_diff).astype(BF16)
        y.append(_dot(mixed, wo_ref[...]))
    x1 = [x_ref[s, :] + _rms(yy, gpa_ref[...]) for s, yy in zip(subs, y)]
    h = [_rms(xx, gpm_ref[...]).astype(BF16) for xx in x1]
    m = [None] * len(subs)
    for c0 in range(0, D_FF, FF_CHUNK):
        ups = [jnp.maximum(_dot(hh, wup_ref[:, c0:c0 + FF_CHUNK]), 0.0) for hh in h]
        for i, up in enumerate(ups):
            part = _dot((up * up).astype(BF16), wdn_ref[c0:c0 + FF_CHUNK, :])
            m[i] = part if m[i] is None else m[i] + part
    for s, xx, mm in zip(subs, x1, m):
        o_ref[s, :] = xx + _rms(mm, gpo_ref[...])


def _post_call(x2, om, od, gate, wbm, wbd, wo, gpa, gpm, wup, wdn, gpo):
    t = x2.shape[0]
    tm = TM_POST
    row = lambda c: pl.BlockSpec((tm, c), lambda i: (i, 0))
    return pl.pallas_call(
        _post_kernel,
        grid=(t // tm,),
        in_specs=[row(D_MODEL), row(om.shape[1]), row(od.shape[1]), row(gate.shape[1]),
                  _const_spec(wbm.shape), _const_spec(wbd.shape), _const_spec(wo.shape),
                  _const_spec((1, D_MODEL)), _const_spec((1, D_MODEL)),
                  _const_spec(wup.shape), _const_spec(wdn.shape), _const_spec((1, D_MODEL))],
        out_specs=row(D_MODEL),
        out_shape=jax.ShapeDtypeStruct((t, D_MODEL), F32),
        compiler_params=pltpu.CompilerParams(
            dimension_semantics=("arbitrary",), vmem_limit_bytes=VMEM_LIMIT),
        name="post",
    )(x2, om, od, gate, wbm, wbd, wo, gpa, gpm, wup, wdn, gpo)


def _rope_tables(seq):
    pos = np.arange(seq, dtype=np.float64)

    def cs(d):
        inv = 1.0 / (ROPE_THETA ** (np.arange(0, d, 2, dtype=np.float64) / d))
        ang = pos[:, None] * inv[None, :]
        return np.cos(ang), np.sin(ang)

    cd, sd = cs(D_DIFF_HEAD)
    cosd = np.tile(cd, (1, 4))
    sind = np.concatenate([-sd, sd, -sd, sd], axis=1)
    cm, sm = cs(D_MLA_ROPE)
    one = np.ones((seq, D_MLA_NOPE))
    zero_n = np.zeros((seq, D_MLA_NOPE))
    zero_p = np.zeros((seq, LANES - D_MLA_NOPE - D_MLA_ROPE))
    cosm = np.concatenate([one, cm, cm, zero_p], axis=1)
    sinm = np.concatenate([zero_n, -sm, sm, zero_p], axis=1)
    return tuple(jnp.asarray(t, dtype=F32) for t in (cosd, sind, cosm, sinm))


def _layout_in_proj(w_in):
    return jnp.swapaxes(w_in, 0, 1).astype(BF16)


def _layout_q_b(w_q_b):
    r = w_q_b.shape[0]
    w = w_q_b.reshape(r, N_MLA_HEADS, D_MLA_NOPE + D_MLA_ROPE)
    nope, rp = w[..., :D_MLA_NOPE], w[..., D_MLA_NOPE:]
    zp = jnp.zeros((r, N_MLA_HEADS, LANES - D_MLA_NOPE - D_MLA_ROPE), w.dtype)
    return jnp.concatenate([nope, rp, zp], axis=-1).reshape(r, N_MLA_HEADS * LANES).astype(BF16)


def _layout_kv_b(w_kv_b):
    r = w_kv_b.shape[0]
    w = w_kv_b.reshape(r, N_MLA_HEADS, D_MLA_NOPE + D_MLA_V)
    kn, v = w[..., :D_MLA_NOPE], w[..., D_MLA_NOPE:]
    zk = jnp.zeros((r, N_MLA_HEADS, LANES - D_MLA_NOPE), w.dtype)
    wk = jnp.concatenate([kn, zk], axis=-1).reshape(r, N_MLA_HEADS * LANES)
    wv = v.reshape(r, N_MLA_HEADS * D_MLA_V)
    return jnp.concatenate([wk, wv], axis=1).astype(BF16)


def kernel(x, pre_attn_g, w_in, q_norm_g, w_q_b, kv_norm_g, w_kv_b, lambda_q1, lambda_k1, lambda_q2, lambda_k2, subln_g, w_br_mla, w_br_diff, w_out, post_attn_g, pre_mlp_g, w_mlp_up, w_mlp_down, post_mlp_g):
    batch, seq, d_model = x.shape
    depth = w_in.shape[0]
    assert d_model == D_MODEL and w_in.shape[1:] == (D_MODEL, D_IN_TOTAL) and w_mlp_up.shape[2] == D_FF
    assert seq % MLA_TQ == 0 and seq % DIFF_TQ == 0 and seq % TM_PROJ == 0
    assert MLA_TQ % MLA_TK == 0 and DIFF_TQ % DIFF_TK == 0 and MLA_TQ % DIAG_TK == 0 and DIFF_TQ % DIAG_TK == 0
    assert (batch * seq) % TM_POST == 0
    cosd, sind, cosm, sinm = _rope_tables(seq)
    x2 = x.reshape(batch * seq, d_model)
    row = lambda v: v.reshape(1, -1).astype(F32)
    for l in range(depth):
        lam_init = 0.8 - 0.6 * math.exp(-0.3 * l)
        qm, km, vm, qd, kd, vd, gate = _proj_call(
            x2, row(pre_attn_g[l]), _layout_in_proj(w_in[l]), row(q_norm_g[l]),
            _layout_q_b(w_q_b[l]), row(kv_norm_g[l]), _layout_kv_b(w_kv_b[l]),
            cosd, sind, cosm, sinm, seq)
        om = _mla_attn_call(qm, km, vm, batch, seq)
        lam_p = jnp.stack([lambda_q1[l], lambda_k1[l], lambda_q2[l], lambda_k2[l]]).astype(F32)
        od = _diff_attn_call(lam_p, row(subln_g[l]), qd, kd, vd, batch, seq, lam_init)
        x2 = _post_call(
            x2, om, od, gate, w_br_mla[l].astype(BF16), w_br_diff[l].astype(BF16),
            w_out[l].astype(BF16), row(post_attn_g[l]), row(pre_mlp_g[l]),
            w_mlp_up[l].astype(BF16), w_mlp_down[l].astype(BF16), row(post_mlp_g[l]))
    return x2.reshape(batch, seq, d_model)
```

```python
import functools
import math

import numpy as np
import jax
import jax.numpy as jnp
from jax import lax
from jax.experimental import pallas as pl
from jax.experimental.pallas import tpu as pltpu

D_MODEL = 1024
N_MLA_HEADS = 8
D_MLA_NOPE = 64
D_MLA_ROPE = 32
D_MLA_V = 64
Q_LORA = 256
KV_LORA = 128
N_DIFF_HEADS = 8
D_DIFF_HEAD = 64
D_FF = 4 * D_MODEL
ROPE_THETA = 10000.0
EPS = 1e-6

LANES = 128
VMEM_LIMIT = 52 * 1024 * 1024

C_DQ = 2 * N_DIFF_HEADS * D_DIFF_HEAD
C_DK = C_DQ
C_DV = N_DIFF_HEADS * 2 * D_DIFF_HEAD
C_GATE = 2 * D_MODEL

R_QA = 0
R_CKV = R_QA + Q_LORA
R_KR = R_CKV + KV_LORA
R_DQ = R_KR + D_MLA_ROPE
R_DK = R_DQ + C_DQ
R_DV = R_DK + C_DK
R_GATE = R_DV + C_DV
D_IN_TOTAL = R_GATE + C_GATE

TM_PROJ = 512
SUB_PROJ = 256
TM_POST = 512
SUB_POST = 256
FF_CHUNK = 1024
MLA_TQ, MLA_TK, MLA_LOOKAHEAD = 512, 256, 3
DIFF_TQ, DIFF_TK, DIFF_LOOKAHEAD = 1024, 512, 2
MLA_HEADS_PER_STEP = 2
DIFF_HEADS_PER_STEP = 2
DIAG_TK = 256

LOG2E = math.log2(math.e)

F32 = jnp.float32
BF16 = jnp.bfloat16


def _rms(x, g):
    return x * lax.rsqrt(jnp.mean(x * x, axis=-1, keepdims=True) + EPS) * g


def _dot(a, b):
    return jnp.dot(a, b, preferred_element_type=F32)


def _dot_nt(a, b):
    return lax.dot_general(a, b, (((1,), (1,)), ((), ())), preferred_element_type=F32)


def _proj_kernel(x_ref, g_ref, wt_ref, qng_ref, wqb_ref, kvg_ref, wkvb_ref,
                 cosd_ref, sind_ref, cosm_ref, sinm_ref,
                 qm_ref, km_ref, vm_ref, qd_ref, kd_ref, vd_ref, gate_ref,
                 *, mla_scale, diff_scale):
    subs = [slice(r0, r0 + SUB_PROJ) for r0 in range(0, x_ref.shape[0], SUB_PROJ)]
    half = N_MLA_HEADS * LANES
    lane = lax.broadcasted_iota(jnp.int32, (1, LANES), 1)
    low_half = lane % D_DIFF_HEAD < D_DIFF_HEAD // 2
    mla_low_half = lane < D_MLA_NOPE + D_MLA_ROPE // 2

    def seg(hh, r0, n):
        return _dot_nt(hh, wt_ref[r0:r0 + n, :])

    def rope_mla(xg, cosm, sinm):
        up = pltpu.roll(xg, LANES - D_MLA_ROPE // 2, 1)
        down = pltpu.roll(xg, D_MLA_ROPE // 2, 1)
        return xg * cosm + jnp.where(mla_low_half, up, down) * sinm

    def rope_group(xg, cosd, sind):
        up = pltpu.roll(xg, LANES - D_DIFF_HEAD // 2, 1)
        down = pltpu.roll(xg, D_DIFF_HEAD // 2, 1)
        return xg * cosd + jnp.where(low_half, up, down) * sind

    h = [_rms(x_ref[s, :], g_ref[...]).astype(BF16) for s in subs]
    lat = [(seg(hh, R_QA, Q_LORA), seg(hh, R_CKV, 2 * LANES), seg(hh, R_GATE, C_GATE)) for hh in h]
    rope_lanes = (lane >= D_MLA_NOPE) & (lane < D_MLA_NOPE + D_MLA_ROPE)

    up_proj = []
    for s, hh, (qa, ckv_kr, gl) in zip(subs, h, lat):
        qn = _rms(qa, qng_ref[...]).astype(BF16)
        kn = _rms(ckv_kr[:, :KV_LORA], kvg_ref[...]).astype(BF16)
        kr = jnp.where(rope_lanes, pltpu.roll(ckv_kr[:, LANES:], D_MLA_NOPE, 1), 0.0)
        krp = rope_mla(kr, cosm_ref[s, :], sinm_ref[s, :])
        up_proj.append((_dot(qn, wqb_ref[...]), _dot(kn, wkvb_ref[...]), krp, seg(hh, R_DQ, C_DQ)))
        gate_ref[s, :] = (1.0 / (1.0 + jnp.exp(-gl))).astype(BF16)

    dks = []
    for s, hh, (qab, kvb, krp, dq) in zip(subs, h, up_proj):
        dks.append(seg(hh, R_DK, C_DK))
        cosm = cosm_ref[s, :]
        sinm = sinm_ref[s, :]
        for j in range(N_MLA_HEADS):
            a = rope_mla(qab[:, j * LANES:(j + 1) * LANES], cosm, sinm)
            qm_ref[s, j * LANES:(j + 1) * LANES] = (a * mla_scale).astype(BF16)
        for j in range(N_MLA_HEADS):
            km_ref[s, j * LANES:(j + 1) * LANES] = (kvb[:, j * LANES:(j + 1) * LANES] + krp).astype(BF16)
        vm_ref[s, :] = kvb[:, half:].astype(BF16)
        for j in range(N_DIFF_HEADS):
            r = rope_group(dq[:, j * LANES:(j + 1) * LANES], cosd_ref[s, :], sind_ref[s, :])
            qd_ref[s, j * LANES:(j + 1) * LANES] = (r * diff_scale).astype(BF16)

    for s, hh, dk in zip(subs, h, dks):
        dv = seg(hh, R_DV, C_DV)
        for j in range(N_DIFF_HEADS):
            kd_ref[s, j * LANES:(j + 1) * LANES] = rope_group(
                dk[:, j * LANES:(j + 1) * LANES], cosd_ref[s, :], sind_ref[s, :]).astype(BF16)
        vd_ref[s, :] = dv.astype(BF16)


def _const_spec(shape):
    return pl.BlockSpec(shape, lambda i: (0,) * len(shape), pipeline_mode=pl.Buffered(1))


def _proj_call(x2, g, wt, qng, wqb, kvg, wkvb, cosd, sind, cosm, sinm, seq):
    t = x2.shape[0]
    tm = TM_PROJ
    tiles_per_seq = seq // tm
    row = lambda c: pl.BlockSpec((tm, c), lambda i: (i, 0))
    tab = pl.BlockSpec((tm, LANES), lambda i: (i % tiles_per_seq, 0))
    widths = (N_MLA_HEADS * LANES, N_MLA_HEADS * LANES, N_MLA_HEADS * D_MLA_V,
              C_DQ, C_DK, C_DV, C_GATE)
    kern = functools.partial(
        _proj_kernel,
        mla_scale=LOG2E / math.sqrt(D_MLA_NOPE + D_MLA_ROPE),
        diff_scale=LOG2E / math.sqrt(D_DIFF_HEAD))
    return pl.pallas_call(
        kern,
        grid=(t // tm,),
        in_specs=[row(D_MODEL), _const_spec((1, D_MODEL)), _const_spec(wt.shape),
                  _const_spec((1, Q_LORA)), _const_spec(wqb.shape),
                  _const_spec((1, KV_LORA)), _const_spec(wkvb.shape),
                  tab, tab, tab, tab],
        out_specs=[row(c) for c in widths],
        out_shape=[jax.ShapeDtypeStruct((t, c), BF16) for c in widths],
        compiler_params=pltpu.CompilerParams(
            dimension_semantics=("arbitrary",), vmem_limit_bytes=VMEM_LIMIT),
        name="proj",
    )(x2, g, wt, qng, wqb, kvg, wkvb, cosd, sind, cosm, sinm)


ONES_ROWS = 16


def _transpose_bf16(x):
    return x.astype(F32).T.astype(BF16)


def _causal_attention_t(load_qts, seq, k_los, k_ref, vts, finish, tq, tk, lookahead):
    n = len(k_los)
    steps = []
    for qi in range(seq // tq):
        q0 = qi * tq
        blocks = [(kb * tk, tk, 0, False) for kb in range(q0 // tk)]
        blocks += [(q0 + d * DIAG_TK, DIAG_TK, d * DIAG_TK, True) for d in range(tq // DIAG_TK)]
        steps += [(q0,) + b + (j == 0, j == len(blocks) - 1) for j, b in enumerate(blocks)]
    units = [(step, i) for step in steps for i in range(n)]
    qts = {}

    def scores(step, i):
        q0, st, klen, c_lo, masked, first, _ = step
        if first and i == 0:
            qts[q0] = load_qts(q0, tq)
        s = _dot(k_ref[st:st + klen, k_los[i]:k_los[i] + LANES], qts[q0][i][:, c_lo:])
        if masked:
            r = lax.broadcasted_iota(jnp.int32, s.shape, 0)
            c = lax.broadcasted_iota(jnp.int32, s.shape, 1)
            s = jnp.where(r <= c, s, -jnp.inf)
        return s

    state = [None] * n
    pending = [scores(*u) for u in units[:lookahead]]
    for t, (step, i) in enumerate(units):
        q0, st, klen, c_lo, masked, first, last = step
        s = pending.pop(0)
        if t + lookahead < len(units):
            pending.append(scores(*units[t + lookahead]))
        vt = vts[i][:, st:st + klen]
        m_blk = jnp.max(s, axis=0, keepdims=True)
        if first:
            m_new = m_blk
            acc = _dot(vt, jnp.exp2(s - m_new).astype(BF16))
        else:
            m, acc = state[i]
            m_new = jnp.maximum(m[:, c_lo:], m_blk)
            p = jnp.exp2(s - m_new).astype(BF16)
            acc_new = jnp.exp2(m[:, c_lo:] - m_new) * acc[:, c_lo:] + _dot(vt, p)
            if c_lo:
                m_new = jnp.concatenate([m[:, :c_lo], m_new], axis=1)
                acc_new = jnp.concatenate([acc[:, :c_lo], acc_new], axis=1)
            acc = acc_new
        state[i] = (m_new, acc)
        if last and i == n - 1:
            finish(q0, [a for _, a in state])


def _mla_attn_kernel(q_ref, k_ref, v_ref, o_ref, vt_ref):
    dv = D_MLA_V
    seq = q_ref.shape[0]
    nh = MLA_HEADS_PER_STEP
    vt = v_ref[...].astype(F32).T
    ones = jnp.ones((ONES_ROWS, seq), BF16)
    for hh in range(nh):
        vt_ref[hh, :dv, :] = vt[hh * dv:(hh + 1) * dv].astype(BF16)
        vt_ref[hh, dv:, :] = ones

    def finish(q0, accs):
        ot = jnp.concatenate([a[:dv] / a[dv:dv + 1] for a in accs], axis=0)
        o_ref[q0:q0 + MLA_TQ, :] = ot.T.astype(BF16)

    def load_qts(q0, tq):
        return [_transpose_bf16(q_ref[q0:q0 + tq, hh * LANES:(hh + 1) * LANES]) for hh in range(nh)]

    _causal_attention_t(load_qts, seq, [hh * LANES for hh in range(nh)], k_ref,
                        [vt_ref.at[hh] for hh in range(nh)], finish, MLA_TQ, MLA_TK, MLA_LOOKAHEAD)


def _diff_attn_kernel(lam_ref, sg_ref, q_ref, k_ref, v_ref, o_ref, vt_ref, *, lam_init):
    dv = 2 * D_DIFF_HEAD
    seq = q_ref.shape[0]
    nh = DIFF_HEADS_PER_STEP
    for hh in range(nh):
        vt_ref[hh, :dv, :] = _transpose_bf16(v_ref[:, hh * LANES:(hh + 1) * LANES])
        vt_ref[hh, dv:, :] = jnp.ones((ONES_ROWS, seq), BF16)
    lp = lam_ref[...]
    lam = (jnp.exp(jnp.sum(lp[0:1] * lp[1:2], axis=-1, keepdims=True))
           - jnp.exp(jnp.sum(lp[2:3] * lp[3:4], axis=-1, keepdims=True)) + lam_init)
    gain = sg_ref[...] * (1.0 - lam_init)

    def finish(q0, accs):
        for hh in range(nh):
            a1, a2 = accs[2 * hh], accs[2 * hh + 1]
            o = a1[:dv] / a1[dv:dv + 1] - lam * (a2[:dv] / a2[dv:dv + 1])
            on = o * lax.rsqrt(jnp.mean(o * o, axis=0, keepdims=True) + EPS)
            o_ref[q0:q0 + DIFF_TQ, hh * LANES:(hh + 1) * LANES] = (on.T * gain).astype(BF16)

    def load_qts(q0, tq):
        out = []
        zero = jnp.zeros((D_DIFF_HEAD, tq), BF16)
        for hh in range(nh):
            qt = _transpose_bf16(q_ref[q0:q0 + tq, hh * LANES:(hh + 1) * LANES])
            out += [jnp.concatenate([qt[:D_DIFF_HEAD], zero], axis=0),
                    jnp.concatenate([zero, qt[D_DIFF_HEAD:]], axis=0)]
        return out

    k_los = [hh * LANES for hh in range(nh) for _ in range(2)]
    vts = [vt_ref.at[hh] for hh in range(nh) for _ in range(2)]
    _causal_attention_t(load_qts, seq, k_los, k_ref, vts, finish, DIFF_TQ, DIFF_TK, DIFF_LOOKAHEAD)


def _attn_kernel(lam_ref, sg_ref, qd_ref, kd_ref, vd_ref, qm_ref, km_ref, vm_ref, od_ref, om_ref,
                 vtd_ref, vtm_ref, *, lam_init):
    _diff_attn_kernel(lam_ref, sg_ref, qd_ref, kd_ref, vd_ref, od_ref, vtd_ref, lam_init=lam_init)
    _mla_attn_kernel(qm_ref, km_ref, vm_ref, om_ref, vtm_ref)


def _attn_call(lam_p, sg, qd, kd, vd, qm, km, vm, batch, seq, lam_init):
    t = qd.shape[0]
    nd, nm = DIFF_HEADS_PER_STEP, MLA_HEADS_PER_STEP
    steps = N_DIFF_HEADS // nd
    assert steps == N_MLA_HEADS // nm
    blk = lambda c: pl.BlockSpec((seq, c), lambda b, j: (b, j))
    return pl.pallas_call(
        functools.partial(_attn_kernel, lam_init=lam_init),
        grid=(batch, steps),
        in_specs=[pl.BlockSpec(lam_p.shape, lambda b, j: (0, 0)),
                  pl.BlockSpec(sg.shape, lambda b, j: (0, 0)),
                  blk(nd * LANES), blk(nd * LANES), blk(nd * LANES),
                  blk(nm * LANES), blk(nm * LANES), blk(nm * D_MLA_V)],
        out_specs=[blk(nd * LANES), blk(nm * D_MLA_V)],
        out_shape=[jax.ShapeDtypeStruct((t, C_DV), BF16),
                   jax.ShapeDtypeStruct((t, N_MLA_HEADS * D_MLA_V), BF16)],
        scratch_shapes=[pltpu.VMEM((nd, 2 * D_DIFF_HEAD + ONES_ROWS, seq), BF16),
                        pltpu.VMEM((nm, D_MLA_V + ONES_ROWS, seq), BF16)],
        compiler_params=pltpu.CompilerParams(
            dimension_semantics=("arbitrary", "arbitrary"), vmem_limit_bytes=VMEM_LIMIT),
        name="attn",
    )(lam_p, sg, qd, kd, vd, qm, km, vm)


def _post_kernel(x_ref, om_ref, od_ref, gate_ref, wbm_ref, wbd_ref, wo_ref, gpa_ref, gpm_ref,
                 wup_ref, wdn_ref, gpo_ref, o_ref):
    subs = [slice(r0, r0 + SUB_POST) for r0 in range(0, x_ref.shape[0], SUB_POST)]
    u = [(_dot(om_ref[s, :], wbm_ref[...]), _dot(od_ref[s, :], wbd_ref[...])) for s in subs]
    y = []
    for s, (u_mla, u_diff) in zip(subs, u):
        mixed = (gate_ref[s, :D_MODEL].astype(F32) * u_mla
                 + gate_ref[s, D_MODEL:].astype(F32) * u_diff).astype(BF16)
        y.append(_dot(mixed, wo_ref[...]))
    x1 = [x_ref[s, :] + _rms(yy, gpa_ref[...]) for s, yy in zip(subs, y)]
    h = [_rms(xx, gpm_ref[...]).astype(BF16) for xx in x1]
    m = [None] * len(subs)
    for c0 in range(0, D_FF, FF_CHUNK):
        ups = [jnp.maximum(_dot(hh, wup_ref[:, c0:c0 + FF_CHUNK]), 0.0) for hh in h]
        for i, up in enumerate(ups):
            part = _dot((up * up).astype(BF16), wdn_ref[c0:c0 + FF_CHUNK, :])
            m[i] = part if m[i] is None else m[i] + part
    for s, xx, mm in zip(subs, x1, m):
        o_ref[s, :] = xx + _rms(mm, gpo_ref[...])


def _post_call(x2, om, od, gate, wbm, wbd, wo, gpa, gpm, wup, wdn, gpo):
    t = x2.shape[0]
    tm = TM_POST
    row = lambda c: pl.BlockSpec((tm, c), lambda i: (i, 0))
    return pl.pallas_call(
        _post_kernel,
        grid=(t // tm,),
        in_specs=[row(D_MODEL), row(om.shape[1]), row(od.shape[1]), row(gate.shape[1]),
                  _const_spec(wbm.shape), _const_spec(wbd.shape), _const_spec(wo.shape),
                  _const_spec((1, D_MODEL)), _const_spec((1, D_MODEL)),
                  _const_spec(wup.shape), _const_spec(wdn.shape), _const_spec((1, D_MODEL))],
        out_specs=row(D_MODEL),
        out_shape=jax.ShapeDtypeStruct((t, D_MODEL), F32),
        compiler_params=pltpu.CompilerParams(
            dimension_semantics=("arbitrary",), vmem_limit_bytes=VMEM_LIMIT),
        name="post",
    )(x2, om, od, gate, wbm, wbd, wo, gpa, gpm, wup, wdn, gpo)


def _rope_tables(seq):
    pos = np.arange(seq, dtype=np.float64)

    def cs(d):
        inv = 1.0 / (ROPE_THETA ** (np.arange(0, d, 2, dtype=np.float64) / d))
        ang = pos[:, None] * inv[None, :]
        return np.cos(ang), np.sin(ang)

    cd, sd = cs(D_DIFF_HEAD)
    cosd = np.tile(cd, (1, 4))
    sind = np.concatenate([-sd, sd, -sd, sd], axis=1)
    cm, sm = cs(D_MLA_ROPE)
    one = np.ones((seq, D_MLA_NOPE))
    zero_n = np.zeros((seq, D_MLA_NOPE))
    zero_p = np.zeros((seq, LANES - D_MLA_NOPE - D_MLA_ROPE))
    cosm = np.concatenate([one, cm, cm, zero_p], axis=1)
    sinm = np.concatenate([zero_n, -sm, sm, zero_p], axis=1)
    return tuple(jnp.asarray(t, dtype=F32) for t in (cosd, sind, cosm, sinm))


def _layout_in_proj(w_in):
    return jnp.swapaxes(w_in, 0, 1).astype(BF16)


def _layout_q_b(w_q_b):
    r = w_q_b.shape[0]
    w = w_q_b.reshape(r, N_MLA_HEADS, D_MLA_NOPE + D_MLA_ROPE)
    nope, rp = w[..., :D_MLA_NOPE], w[..., D_MLA_NOPE:]
    zp = jnp.zeros((r, N_MLA_HEADS, LANES - D_MLA_NOPE - D_MLA_ROPE), w.dtype)
    return jnp.concatenate([nope, rp, zp], axis=-1).reshape(r, N_MLA_HEADS * LANES).astype(BF16)


def _layout_kv_b(w_kv_b):
    r = w_kv_b.shape[0]
    w = w_kv_b.reshape(r, N_MLA_HEADS, D_MLA_NOPE + D_MLA_V)
    kn, v = w[..., :D_MLA_NOPE], w[..., D_MLA_NOPE:]
    zk = jnp.zeros((r, N_MLA_HEADS, LANES - D_MLA_NOPE), w.dtype)
    wk = jnp.concatenate([kn, zk], axis=-1).reshape(r, N_MLA_HEADS * LANES)
    wv = v.reshape(r, N_MLA_HEADS * D_MLA_V)
    return jnp.concatenate([wk, wv], axis=1).astype(BF16)


def kernel(x, pre_attn_g, w_in, q_norm_g, w_q_b, kv_norm_g, w_kv_b, lambda_q1, lambda_k1, lambda_q2, lambda_k2, subln_g, w_br_mla, w_br_diff, w_out, post_attn_g, pre_mlp_g, w_mlp_up, w_mlp_down, post_mlp_g):
    batch, seq, d_model = x.shape
    depth = w_in.shape[0]
    assert d_model == D_MODEL and w_in.shape[1:] == (D_MODEL, D_IN_TOTAL) and w_mlp_up.shape[2] == D_FF
    assert seq % MLA_TQ == 0 and seq % DIFF_TQ == 0 and seq % TM_PROJ == 0
    assert MLA_TQ % MLA_TK == 0 and DIFF_TQ % DIFF_TK == 0 and MLA_TQ % DIAG_TK == 0 and DIFF_TQ % DIAG_TK == 0
    assert (batch * seq) % TM_POST == 0
    cosd, sind, cosm, sinm = _rope_tables(seq)
    x2 = x.reshape(batch * seq, d_model)
    row = lambda v: v.reshape(1, -1).astype(F32)
    for l in range(depth):
        lam_init = 0.8 - 0.6 * math.exp(-0.3 * l)
        qm, km, vm, qd, kd, vd, gate = _proj_call(
            x2, row(pre_attn_g[l]), _layout_in_proj(w_in[l]), row(q_norm_g[l]),
            _layout_q_b(w_q_b[l]), row(kv_norm_g[l]), _layout_kv_b(w_kv_b[l]),
            cosd, sind, cosm, sinm, seq)
        lam_p = jnp.stack([lambda_q1[l], lambda_k1[l], lambda_q2[l], lambda_k2[l]]).astype(F32)
        od, om = _attn_call(lam_p, row(subln_g[l]), qd, kd, vd, qm, km, vm, batch, seq, lam_init)
        x2 = _post_call(
            x2, om, od, gate, w_br_mla[l].astype(BF16), w_br_diff[l].astype(BF16),
            w_out[l].astype(BF16), row(post_attn_g[l]), row(pre_mlp_g[l]),
            w_mlp_up[l].astype(BF16), w_mlp_down[l].astype(BF16), row(post_mlp_g[l]))
    return x2.reshape(batch, seq, d_model)
```

```python
import functools
import math

import numpy as np
import jax
import jax.numpy as jnp
from jax import lax
from jax.experimental import pallas as pl
from jax.experimental.pallas import tpu as pltpu

D_MODEL = 1024
N_MLA_HEADS = 8
D_MLA_NOPE = 64
D_MLA_ROPE = 32
D_MLA_V = 64
Q_LORA = 256
KV_LORA = 128
N_DIFF_HEADS = 8
D_DIFF_HEAD = 64
D_FF = 4 * D_MODEL
ROPE_THETA = 10000.0
EPS = 1e-6

LANES = 128
VMEM_LIMIT = 52 * 1024 * 1024

C_DQ = 2 * N_DIFF_HEADS * D_DIFF_HEAD
C_DK = C_DQ
C_DV = N_DIFF_HEADS * 2 * D_DIFF_HEAD
C_GATE = 2 * D_MODEL

R_QA = 0
R_CKV = R_QA + Q_LORA
R_KR = R_CKV + KV_LORA
R_DQ = R_KR + D_MLA_ROPE
R_DK = R_DQ + C_DQ
R_DV = R_DK + C_DK
R_GATE = R_DV + C_DV
D_IN_TOTAL = R_GATE + C_GATE

TM_PROJ = 512
SUB_PROJ = 256
TM_POST = 512
SUB_POST = 256
FF_CHUNK = 1024
MLA_TQ, MLA_TK, MLA_LOOKAHEAD = 512, 256, 3
DIFF_TQ, DIFF_TK, DIFF_LOOKAHEAD = 1024, 512, 2
MLA_HEADS_PER_STEP = 2
DIFF_HEADS_PER_STEP = 2
DIAG_TK = 256

LOG2E = math.log2(math.e)

F32 = jnp.float32
BF16 = jnp.bfloat16


def _rms(x, g):
    return x * lax.rsqrt(jnp.mean(x * x, axis=-1, keepdims=True) + EPS) * g


def _dot(a, b):
    return jnp.dot(a, b, preferred_element_type=F32)


def _dot_nt(a, b):
    return lax.dot_general(a, b, (((1,), (1,)), ((), ())), preferred_element_type=F32)


def _proj_kernel(*refs, n_cast, mla_scale, diff_scale):
    (x_ref, g_ref, wt_ref, qng_ref, wqb_ref, kvg_ref, wkvb_ref,
     cosd_ref, sind_ref, cosm_ref, sinm_ref) = refs[:11]
    qm_ref, km_ref, vm_ref, qd_ref, kd_ref, vd_ref, gate_ref = refs[11 + n_cast:18 + n_cast]
    for src_ref, dst_ref in zip(refs[11:11 + n_cast], refs[18 + n_cast:]):
        dst_ref[...] = src_ref[...].astype(BF16)

    subs = [slice(r0, r0 + SUB_PROJ) for r0 in range(0, x_ref.shape[0], SUB_PROJ)]
    half = N_MLA_HEADS * LANES
    lane = lax.broadcasted_iota(jnp.int32, (1, LANES), 1)
    low_half = lane % D_DIFF_HEAD < D_DIFF_HEAD // 2
    mla_low_half = lane < D_MLA_NOPE + D_MLA_ROPE // 2

    def seg(hh, r0, n):
        return _dot_nt(hh, wt_ref[r0:r0 + n, :])

    def rope_mla(xg, cosm, sinm):
        up = pltpu.roll(xg, LANES - D_MLA_ROPE // 2, 1)
        down = pltpu.roll(xg, D_MLA_ROPE // 2, 1)
        return xg * cosm + jnp.where(mla_low_half, up, down) * sinm

    def rope_group(xg, cosd, sind):
        up = pltpu.roll(xg, LANES - D_DIFF_HEAD // 2, 1)
        down = pltpu.roll(xg, D_DIFF_HEAD // 2, 1)
        return xg * cosd + jnp.where(low_half, up, down) * sind

    h = [_rms(x_ref[s, :], g_ref[...]).astype(BF16) for s in subs]
    lat = [(seg(hh, R_QA, Q_LORA), seg(hh, R_CKV, 2 * LANES), seg(hh, R_GATE, C_GATE)) for hh in h]
    rope_lanes = (lane >= D_MLA_NOPE) & (lane < D_MLA_NOPE + D_MLA_ROPE)

    up_proj = []
    for s, hh, (qa, ckv_kr, gl) in zip(subs, h, lat):
        qn = _rms(qa, qng_ref[...]).astype(BF16)
        kn = _rms(ckv_kr[:, :KV_LORA], kvg_ref[...]).astype(BF16)
        kr = jnp.where(rope_lanes, pltpu.roll(ckv_kr[:, LANES:], D_MLA_NOPE, 1), 0.0)
        krp = rope_mla(kr, cosm_ref[s, :], sinm_ref[s, :])
        up_proj.append((_dot(qn, wqb_ref[...]), _dot(kn, wkvb_ref[...]), krp, seg(hh, R_DQ, C_DQ)))
        gate_ref[s, :] = (1.0 / (1.0 + jnp.exp(-gl))).astype(BF16)

    dks = []
    for s, hh, (qab, kvb, krp, dq) in zip(subs, h, up_proj):
        dks.append(seg(hh, R_DK, C_DK))
        cosm = cosm_ref[s, :]
        sinm = sinm_ref[s, :]
        for j in range(N_MLA_HEADS):
            a = rope_mla(qab[:, j * LANES:(j + 1) * LANES], cosm, sinm)
            qm_ref[s, j * LANES:(j + 1) * LANES] = (a * mla_scale).astype(BF16)
        for j in range(N_MLA_HEADS):
            km_ref[s, j * LANES:(j + 1) * LANES] = (kvb[:, j * LANES:(j + 1) * LANES] + krp).astype(BF16)
        vm_ref[s, :] = kvb[:, half:].astype(BF16)
        for j in range(N_DIFF_HEADS):
            r = rope_group(dq[:, j * LANES:(j + 1) * LANES], cosd_ref[s, :], sind_ref[s, :])
            qd_ref[s, j * LANES:(j + 1) * LANES] = (r * diff_scale).astype(BF16)

    for s, hh, dk in zip(subs, h, dks):
        dv = seg(hh, R_DV, C_DV)
        for j in range(N_DIFF_HEADS):
            kd_ref[s, j * LANES:(j + 1) * LANES] = rope_group(
                dk[:, j * LANES:(j + 1) * LANES], cosd_ref[s, :], sind_ref[s, :]).astype(BF16)
        vd_ref[s, :] = dv.astype(BF16)


def _const_spec(shape):
    return pl.BlockSpec(shape, lambda i: (0,) * len(shape), pipeline_mode=pl.Buffered(1))


def _proj_call(x2, g, wt, qng, wqb, kvg, wkvb, cosd, sind, cosm, sinm, later_ws, l, seq):
    t = x2.shape[0]
    tm = TM_PROJ
    tiles_per_seq = seq // tm
    steps = t // tm
    slabs = [w.shape[1] // steps for w in later_ws]
    assert all(r % 16 == 0 and r * steps == w.shape[1] for r, w in zip(slabs, later_ws))
    slab_in = [pl.BlockSpec((None, r, w.shape[2]), lambda i: (l, i, 0)) for r, w in zip(slabs, later_ws)]
    slab_out = [pl.BlockSpec((r, w.shape[2]), lambda i: (i, 0)) for r, w in zip(slabs, later_ws)]
    row = lambda c: pl.BlockSpec((tm, c), lambda i: (i, 0))
    tab = pl.BlockSpec((tm, LANES), lambda i: (i % tiles_per_seq, 0))
    widths = (N_MLA_HEADS * LANES, N_MLA_HEADS * LANES, N_MLA_HEADS * D_MLA_V,
              C_DQ, C_DK, C_DV, C_GATE)
    kern = functools.partial(
        _proj_kernel,
        n_cast=len(later_ws),
        mla_scale=LOG2E / math.sqrt(D_MLA_NOPE + D_MLA_ROPE),
        diff_scale=LOG2E / math.sqrt(D_DIFF_HEAD))
    return pl.pallas_call(
        kern,
        grid=(t // tm,),
        in_specs=[row(D_MODEL), _const_spec((1, D_MODEL)), _const_spec(wt.shape),
                  _const_spec((1, Q_LORA)), _const_spec(wqb.shape),
                  _const_spec((1, KV_LORA)), _const_spec(wkvb.shape),
                  tab, tab, tab, tab] + slab_in,
        out_specs=[row(c) for c in widths] + slab_out,
        out_shape=[jax.ShapeDtypeStruct((t, c), BF16) for c in widths]
        + [jax.ShapeDtypeStruct(w.shape[1:], BF16) for w in later_ws],
        compiler_params=pltpu.CompilerParams(
            dimension_semantics=("arbitrary",), vmem_limit_bytes=VMEM_LIMIT),
        name="proj",
    )(x2, g, wt, qng, wqb, kvg, wkvb, cosd, sind, cosm, sinm, *later_ws)


ONES_ROWS = 16


def _transpose_bf16(x):
    return x.T


def _causal_attention_t(load_qts, seq, k_los, k_ref, vts, finish, tq, tk, lookahead):
    n = len(k_los)
    steps = []
    for qi in range(seq // tq):
        q0 = qi * tq
        blocks = [(kb * tk, tk, 0, False) for kb in range(q0 // tk)]
        blocks += [(q0 + d * DIAG_TK, DIAG_TK, d * DIAG_TK, True) for d in range(tq // DIAG_TK)]
        steps += [(q0,) + b + (j == 0, j == len(blocks) - 1) for j, b in enumerate(blocks)]
    units = [(step, i) for step in steps for i in range(n)]
    qts = {}

    def scores(step, i):
        q0, st, klen, c_lo, masked, first, _ = step
        if first and i == 0:
            qts[q0] = load_qts(q0, tq)
        s = _dot(k_ref[st:st + klen, k_los[i]:k_los[i] + LANES], qts[q0][i][:, c_lo:])
        if masked:
            r = lax.broadcasted_iota(jnp.int32, s.shape, 0)
            c = lax.broadcasted_iota(jnp.int32, s.shape, 1)
            s = jnp.where(r <= c, s, -jnp.inf)
        return s

    state = [None] * n
    pending = [scores(*u) for u in units[:lookahead]]
    for t, (step, i) in enumerate(units):
        q0, st, klen, c_lo, masked, first, last = step
        s = pending.pop(0)
        if t + lookahead < len(units):
            pending.append(scores(*units[t + lookahead]))
        vt = vts[i][:, st:st + klen]
        m_blk = jnp.max(s, axis=0, keepdims=True)
        if first:
            m_new = m_blk
            acc = _dot(vt, jnp.exp2(s - m_new).astype(BF16))
        else:
            m, acc = state[i]
            m_new = jnp.maximum(m[:, c_lo:], m_blk)
            p = jnp.exp2(s - m_new).astype(BF16)
            acc_new = jnp.exp2(m[:, c_lo:] - m_new) * acc[:, c_lo:] + _dot(vt, p)
            if c_lo:
                m_new = jnp.concatenate([m[:, :c_lo], m_new], axis=1)
                acc_new = jnp.concatenate([acc[:, :c_lo], acc_new], axis=1)
            acc = acc_new
        state[i] = (m_new, acc)
        if last and i == n - 1:
            finish(q0, [a for _, a in state])


def _mla_attn_kernel(q_ref, k_ref, v_ref, o_ref, vt_ref):
    dv = D_MLA_V
    seq = q_ref.shape[0]
    nh = MLA_HEADS_PER_STEP
    vt = v_ref[...].astype(F32).T
    ones = jnp.ones((ONES_ROWS, seq), BF16)
    for hh in range(nh):
        vt_ref[hh, :dv, :] = vt[hh * dv:(hh + 1) * dv].astype(BF16)
        vt_ref[hh, dv:, :] = ones

    def finish(q0, accs):
        ot = jnp.concatenate([a[:dv] / a[dv:dv + 1] for a in accs], axis=0)
        o_ref[q0:q0 + MLA_TQ, :] = ot.T.astype(BF16)

    def load_qts(q0, tq):
        return [_transpose_bf16(q_ref[q0:q0 + tq, hh * LANES:(hh + 1) * LANES]) for hh in range(nh)]

    _causal_attention_t(load_qts, seq, [hh * LANES for hh in range(nh)], k_ref,
                        [vt_ref.at[hh] for hh in range(nh)], finish, MLA_TQ, MLA_TK, MLA_LOOKAHEAD)


def _diff_attn_kernel(lam_ref, sg_ref, q_ref, k_ref, v_ref, o_ref, vt_ref, *, lam_init):
    dv = 2 * D_DIFF_HEAD
    seq = q_ref.shape[0]
    nh = DIFF_HEADS_PER_STEP
    for hh in range(nh):
        vt_ref[hh, :dv, :] = _transpose_bf16(v_ref[:, hh * LANES:(hh + 1) * LANES])
        vt_ref[hh, dv:, :] = jnp.ones((ONES_ROWS, seq), BF16)
    lp = lam_ref[...]
    lam = (jnp.exp(jnp.sum(lp[0:1] * lp[1:2], axis=-1, keepdims=True))
           - jnp.exp(jnp.sum(lp[2:3] * lp[3:4], axis=-1, keepdims=True)) + lam_init)
    gain = sg_ref[...] * (1.0 - lam_init)

    def finish(q0, accs):
        for hh in range(nh):
            a1, a2 = accs[2 * hh], accs[2 * hh + 1]
            o = a1[:dv] / a1[dv:dv + 1] - lam * (a2[:dv] / a2[dv:dv + 1])
            on = o * lax.rsqrt(jnp.mean(o * o, axis=0, keepdims=True) + EPS)
            o_ref[q0:q0 + DIFF_TQ, hh * LANES:(hh + 1) * LANES] = (on.T * gain).astype(BF16)

    def load_qts(q0, tq):
        out = []
        zero = jnp.zeros((D_DIFF_HEAD, tq), BF16)
        for hh in range(nh):
            qt = _transpose_bf16(q_ref[q0:q0 + tq, hh * LANES:(hh + 1) * LANES])
            out += [jnp.concatenate([qt[:D_DIFF_HEAD], zero], axis=0),
                    jnp.concatenate([zero, qt[D_DIFF_HEAD:]], axis=0)]
        return out

    k_los = [hh * LANES for hh in range(nh) for _ in range(2)]
    vts = [vt_ref.at[hh] for hh in range(nh) for _ in range(2)]
    _causal_attention_t(load_qts, seq, k_los, k_ref, vts, finish, DIFF_TQ, DIFF_TK, DIFF_LOOKAHEAD)


def _attn_kernel(lam_ref, sg_ref, qd_ref, kd_ref, vd_ref, qm_ref, km_ref, vm_ref, od_ref, om_ref,
                 vtd_ref, vtm_ref, *, lam_init):
    _diff_attn_kernel(lam_ref, sg_ref, qd_ref, kd_ref, vd_ref, od_ref, vtd_ref, lam_init=lam_init)
    _mla_attn_kernel(qm_ref, km_ref, vm_ref, om_ref, vtm_ref)


def _attn_call(lam_p, sg, qd, kd, vd, qm, km, vm, batch, seq, lam_init):
    t = qd.shape[0]
    nd, nm = DIFF_HEADS_PER_STEP, MLA_HEADS_PER_STEP
    steps = N_DIFF_HEADS // nd
    assert steps == N_MLA_HEADS // nm
    blk = lambda c: pl.BlockSpec((seq, c), lambda b, j: (b, j))
    return pl.pallas_call(
        functools.partial(_attn_kernel, lam_init=lam_init),
        grid=(batch, steps),
        in_specs=[pl.BlockSpec(lam_p.shape, lambda b, j: (0, 0)),
                  pl.BlockSpec(sg.shape, lambda b, j: (0, 0)),
                  blk(nd * LANES), blk(nd * LANES), blk(nd * LANES),
                  blk(nm * LANES), blk(nm * LANES), blk(nm * D_MLA_V)],
        out_specs=[blk(nd * LANES), blk(nm * D_MLA_V)],
        out_shape=[jax.ShapeDtypeStruct((t, C_DV), BF16),
                   jax.ShapeDtypeStruct((t, N_MLA_HEADS * D_MLA_V), BF16)],
        scratch_shapes=[pltpu.VMEM((nd, 2 * D_DIFF_HEAD + ONES_ROWS, seq), BF16),
                        pltpu.VMEM((nm, D_MLA_V + ONES_ROWS, seq), BF16)],
        compiler_params=pltpu.CompilerParams(
            dimension_semantics=("arbitrary", "arbitrary"), vmem_limit_bytes=VMEM_LIMIT),
        name="attn",
    )(lam_p, sg, qd, kd, vd, qm, km, vm)


def _post_kernel(x_ref, om_ref, od_ref, gate_ref, wbm_ref, wbd_ref, wo_ref, gpa_ref, gpm_ref,
                 wup_ref, wdn_ref, gpo_ref, o_ref):
    subs = [slice(r0, r0 + SUB_POST) for r0 in range(0, x_ref.shape[0], SUB_POST)]
    u = [(_dot(om_ref[s, :], wbm_ref[...]), _dot(od_ref[s, :], wbd_ref[...])) for s in subs]
    y = []
    for s, (u_mla, u_diff) in zip(subs, u):
        mixed = (gate_ref[s, :D_MODEL].astype(F32) * u_mla
                 + gate_ref[s, D_MODEL:].astype(F32) * u_diff).astype(BF16)
        y.append(_dot(mixed, wo_ref[...]))
    x1 = [x_ref[s, :] + _rms(yy, gpa_ref[...]) for s, yy in zip(subs, y)]
    h = [_rms(xx, gpm_ref[...]).astype(BF16) for xx in x1]
    m = [None] * len(subs)
    for c0 in range(0, D_FF, FF_CHUNK):
        ups = [jnp.maximum(_dot(hh, wup_ref[:, c0:c0 + FF_CHUNK]), 0.0) for hh in h]
        for i, up in enumerate(ups):
            part = _dot((up * up).astype(BF16), wdn_ref[c0:c0 + FF_CHUNK, :])
            m[i] = part if m[i] is None else m[i] + part
    for s, xx, mm in zip(subs, x1, m):
        o_ref[s, :] = xx + _rms(mm, gpo_ref[...])


def _post_call(x2, om, od, gate, wbm, wbd, wo, gpa, gpm, wup, wdn, gpo):
    t = x2.shape[0]
    tm = TM_POST
    row = lambda c: pl.BlockSpec((tm, c), lambda i: (i, 0))
    return pl.pallas_call(
        _post_kernel,
        grid=(t // tm,),
        in_specs=[row(D_MODEL), row(om.shape[1]), row(od.shape[1]), row(gate.shape[1]),
                  _const_spec(wbm.shape), _const_spec(wbd.shape), _const_spec(wo.shape),
                  _const_spec((1, D_MODEL)), _const_spec((1, D_MODEL)),
                  _const_spec(wup.shape), _const_spec(wdn.shape), _const_spec((1, D_MODEL))],
        out_specs=row(D_MODEL),
        out_shape=jax.ShapeDtypeStruct((t, D_MODEL), F32),
        compiler_params=pltpu.CompilerParams(
            dimension_semantics=("arbitrary",), vmem_limit_bytes=VMEM_LIMIT),
        name="post",
    )(x2, om, od, gate, wbm, wbd, wo, gpa, gpm, wup, wdn, gpo)


def _rope_tables(seq):
    pos = np.arange(seq, dtype=np.float64)

    def cs(d):
        inv = 1.0 / (ROPE_THETA ** (np.arange(0, d, 2, dtype=np.float64) / d))
        ang = pos[:, None] * inv[None, :]
        return np.cos(ang), np.sin(ang)

    cd, sd = cs(D_DIFF_HEAD)
    cosd = np.tile(cd, (1, 4))
    sind = np.concatenate([-sd, sd, -sd, sd], axis=1)
    cm, sm = cs(D_MLA_ROPE)
    one = np.ones((seq, D_MLA_NOPE))
    zero_n = np.zeros((seq, D_MLA_NOPE))
    zero_p = np.zeros((seq, LANES - D_MLA_NOPE - D_MLA_ROPE))
    cosm = np.concatenate([one, cm, cm, zero_p], axis=1)
    sinm = np.concatenate([zero_n, -sm, sm, zero_p], axis=1)
    return tuple(jnp.asarray(t, dtype=F32) for t in (cosd, sind, cosm, sinm))


def _layout_in_proj(w_in):
    return jnp.swapaxes(w_in, 0, 1).astype(BF16)


def _layout_q_b(w_q_b):
    r = w_q_b.shape[0]
    w = w_q_b.reshape(r, N_MLA_HEADS, D_MLA_NOPE + D_MLA_ROPE)
    nope, rp = w[..., :D_MLA_NOPE], w[..., D_MLA_NOPE:]
    zp = jnp.zeros((r, N_MLA_HEADS, LANES - D_MLA_NOPE - D_MLA_ROPE), w.dtype)
    return jnp.concatenate([nope, rp, zp], axis=-1).reshape(r, N_MLA_HEADS * LANES).astype(BF16)


def _layout_kv_b(w_kv_b):
    r = w_kv_b.shape[0]
    w = w_kv_b.reshape(r, N_MLA_HEADS, D_MLA_NOPE + D_MLA_V)
    kn, v = w[..., :D_MLA_NOPE], w[..., D_MLA_NOPE:]
    zk = jnp.zeros((r, N_MLA_HEADS, LANES - D_MLA_NOPE), w.dtype)
    wk = jnp.concatenate([kn, zk], axis=-1).reshape(r, N_MLA_HEADS * LANES)
    wv = v.reshape(r, N_MLA_HEADS * D_MLA_V)
    return jnp.concatenate([wk, wv], axis=1).astype(BF16)


def kernel(x, pre_attn_g, w_in, q_norm_g, w_q_b, kv_norm_g, w_kv_b, lambda_q1, lambda_k1, lambda_q2, lambda_k2, subln_g, w_br_mla, w_br_diff, w_out, post_attn_g, pre_mlp_g, w_mlp_up, w_mlp_down, post_mlp_g):
    batch, seq, d_model = x.shape
    depth = w_in.shape[0]
    assert d_model == D_MODEL and w_in.shape[1:] == (D_MODEL, D_IN_TOTAL) and w_mlp_up.shape[2] == D_FF
    assert seq % MLA_TQ == 0 and seq % DIFF_TQ == 0 and seq % TM_PROJ == 0
    assert MLA_TQ % MLA_TK == 0 and DIFF_TQ % DIFF_TK == 0 and MLA_TQ % DIAG_TK == 0 and DIFF_TQ % DIAG_TK == 0
    assert (batch * seq) % TM_POST == 0
    cosd, sind, cosm, sinm = _rope_tables(seq)
    x2 = x.reshape(batch * seq, d_model)
    row = lambda v: v.reshape(1, -1).astype(F32)
    for l in range(depth):
        lam_init = 0.8 - 0.6 * math.exp(-0.3 * l)
        qm, km, vm, qd, kd, vd, gate, wup_bf, wdn_bf, wbm_bf, wbd_bf, wo_bf = _proj_call(
            x2, row(pre_attn_g[l]), _layout_in_proj(w_in[l]), row(q_norm_g[l]),
            _layout_q_b(w_q_b[l]), row(kv_norm_g[l]), _layout_kv_b(w_kv_b[l]),
            cosd, sind, cosm, sinm, (w_mlp_up, w_mlp_down, w_br_mla, w_br_diff, w_out), l, seq)
        lam_p = jnp.stack([lambda_q1[l], lambda_k1[l], lambda_q2[l], lambda_k2[l]]).astype(F32)
        od, om = _attn_call(lam_p, row(subln_g[l]), qd, kd, vd, qm, km, vm, batch, seq, lam_init)
        x2 = _post_call(
            x2, om, od, gate, wbm_bf, wbd_bf, wo_bf, row(post_attn_g[l]), row(pre_mlp_g[l]),
            wup_bf, wdn_bf, row(post_mlp_g[l]))
    return x2.reshape(batch, seq, d_model)
```
